```python
import jax, jax.numpy as jnp
from jax import lax
import numpy as np

D_MODEL = 2048
BATCH = 4
SEQ = 8192
DEPTH = 4

MIX_WIDTH = D_MODEL
ATT_HEADS = 8
ATT_KV_HEADS = 2
ATT_WIDTH = MIX_WIDTH // 2
ATT_HEAD_DIM = ATT_WIDTH // ATT_HEADS
KV_WIDTH = ATT_KV_HEADS * ATT_HEAD_DIM
WINDOW = 128
ATT_BLOCK = 128
ROPE_THETA = 10000.0
M_WIDTH = MIX_WIDTH - ATT_WIDTH
M_HEADS = 4
M_HEAD_DIM = M_WIDTH // M_HEADS
M_CHUNK = 128
CONV_WIDTH = 5
D_FF = ((8 * D_MODEL // 3 + 255) // 256) * 256
EPS = 1e-6
IN_SIZES = (ATT_WIDTH, KV_WIDTH, KV_WIDTH, M_WIDTH, M_WIDTH, M_WIDTH, M_WIDTH, 4 * M_HEADS)
IN_WIDTH = sum(IN_SIZES)
IN_SPLITS = tuple(int(s) for s in np.cumsum(IN_SIZES[:-1]))

kernel_name = "hymba_style_mlstm_swa_macaron_encoder"


def rms_norm(x, g):
    xf = x.astype(jnp.float32)
    y = xf * lax.rsqrt(jnp.mean(xf * xf, axis=-1, keepdims=True) + EPS)
    return (y * g.astype(jnp.float32)).astype(x.dtype)


def swiglu(x, w_gate, w_up, w_down):
    return (jax.nn.silu(x @ w_gate) * (x @ w_up)) @ w_down


def rope(x, pos):
    half = x.shape[-1] // 2
    inv_freq = ROPE_THETA ** (-jnp.arange(half, dtype=jnp.float32) / half)
    ang = pos[:, None] * inv_freq[None, :]
    cos = jnp.cos(ang)[None, :, None, :]
    sin = jnp.sin(ang)[None, :, None, :]
    xf = x.astype(jnp.float32)
    x1, x2 = xf[..., :half], xf[..., half:]
    out = jnp.concatenate([x1 * cos - x2 * sin, x2 * cos + x1 * sin], axis=-1)
    return out.astype(x.dtype)


def window_attention(q, k, v, sink):
    B, S, _, D = q.shape
    nb = S // ATT_BLOCK
    G = ATT_HEADS // ATT_KV_HEADS
    qb = q.reshape(B, nb, ATT_BLOCK, ATT_KV_HEADS, G, D)

    def neighbours(t):
        tp = jnp.pad(t, ((0, 0), (ATT_BLOCK, ATT_BLOCK), (0, 0), (0, 0)))
        tp = tp.reshape(B, nb + 2, ATT_BLOCK, ATT_KV_HEADS, D)
        return jnp.concatenate([tp[:, :-2], tp[:, 1:-1], tp[:, 2:]], axis=2)

    kb, vb = neighbours(k), neighbours(v)
    scores = jnp.einsum('bnqhgd,bnkhd->bnhgqk', qb, kb).astype(jnp.float32) * (D ** -0.5)
    r = jnp.arange(ATT_BLOCK)[:, None]
    c = jnp.arange(3 * ATT_BLOCK)[None, :]
    kpos = (jnp.arange(nb)[:, None, None] - 1) * ATT_BLOCK + c[None]
    mask = (jnp.abs(c - ATT_BLOCK - r) <= WINDOW)[None] & (kpos >= 0) & (kpos < S)
    scores = jnp.where(mask[None, :, None, None], scores, -jnp.inf)
    sink_l = sink.astype(jnp.float32).reshape(ATT_KV_HEADS, G)[None, None, :, :, None, None]
    m = jnp.maximum(scores.max(axis=-1, keepdims=True), sink_l)
    p = jnp.exp(scores - m)
    probs = p / (p.sum(axis=-1, keepdims=True) + jnp.exp(sink_l - m))
    out = jnp.einsum('bnhgqk,bnkhd->bnqhgd', probs.astype(v.dtype), vb)
    return out.reshape(B, S, ATT_HEADS * D)


def mlstm_scan(q, k, v, ig, lf):
    B, S, H, D = q.shape
    L = M_CHUNK
    nc = S // L
    to_c = lambda t: t.reshape(B, nc, L, H, D).transpose(1, 0, 3, 2, 4)
    to_cg = lambda g: g.reshape(B, nc, L, H).transpose(1, 0, 3, 2)
    tri = jnp.tril(jnp.ones((L, L), dtype=bool))

    def step(carry, xs):
        C, n, m = carry
        qc, kc, vc, igc, lfc = xs
        b = jnp.cumsum(lfc, axis=-1)
        log_d = b[..., :, None] - b[..., None, :] + igc[..., None, :]
        log_d = jnp.where(tri, log_d, -jnp.inf)
        log_inter = b + m[..., None]
        m_t = jnp.maximum(log_inter, log_d.max(axis=-1))
        d_mat = jnp.exp(log_d - m_t[..., None])
        inter = jnp.exp(log_inter - m_t)
        s = jnp.einsum('bhtd,bhsd->bhts', qc, kc) * d_mat
        num = jnp.einsum('bhts,bhsd->bhtd', s, vc) + inter[..., None] * jnp.einsum('bhvk,bhtk->bhtv', C, qc)
        den = s.sum(axis=-1) + inter * jnp.einsum('bhk,bhtk->bht', n, qc)
        h = num / jnp.maximum(jnp.abs(den), jnp.exp(-m_t))[..., None]
        b_last = b[..., -1]
        log_w = b_last[..., None] - b + igc
        m_new = jnp.maximum(b_last + m, log_w.max(axis=-1))
        w = jnp.exp(log_w - m_new[..., None])
        decay = jnp.exp(b_last + m - m_new)
        C_new = decay[..., None, None] * C + jnp.einsum('bhsv,bhsk->bhvk', w[..., None] * vc, kc)
        n_new = decay[..., None] * n + jnp.einsum('bhs,bhsk->bhk', w, kc)
        return (C_new, n_new, m_new), h

    init = (jnp.zeros((B, H, D, D), jnp.float32), jnp.zeros((B, H, D), jnp.float32),
            jnp.zeros((B, H), jnp.float32))
    _, hs = lax.scan(step, init, (to_c(q), to_c(k), to_c(v), to_cg(ig), to_cg(lf)))
    return hs.transpose(1, 0, 3, 2, 4).reshape(B, S, H, D)


def depthwise_conv_centred(x, w):
    C = x.shape[-1]
    pad = CONV_WIDTH // 2
    return lax.conv_general_dilated(x, w[:, None, :].astype(x.dtype), window_strides=(1,),
                                    padding=[(pad, pad)], dimension_numbers=('NWC', 'WIO', 'NWC'),
                                    feature_group_count=C)


def hybrid_mixer(h, w_in, b_gate, conv_w, sink, m_norm, w_out, pos):
    B, S, _ = h.shape
    proj = h @ w_in
    qa, ka, va, qm, km, vm, om, gates = jnp.split(proj, IN_SPLITS, axis=-1)
    qa = rope(qa.reshape(B, S, ATT_HEADS, ATT_HEAD_DIM), pos)
    ka = rope(ka.reshape(B, S, ATT_KV_HEADS, ATT_HEAD_DIM), pos)
    va = va.reshape(B, S, ATT_KV_HEADS, ATT_HEAD_DIM)
    y_att = window_attention(qa, ka, va, sink)
    qk = jax.nn.silu(depthwise_conv_centred(jnp.concatenate([qm, km], axis=-1), conv_w))
    qm, km = qk[..., :M_WIDTH], qk[..., M_WIDTH:]
    qm = qm.astype(jnp.float32).reshape(B, S, M_HEADS, M_HEAD_DIM)
    km = km.astype(jnp.float32).reshape(B, S, M_HEADS, M_HEAD_DIM) * (M_HEAD_DIM ** -0.5)
    vm = vm.astype(jnp.float32).reshape(B, S, M_HEADS, M_HEAD_DIM)
    g = gates.astype(jnp.float32) + b_gate.astype(jnp.float32)
    ig_f, ig_b, fg_f, fg_b = jnp.split(g, 4, axis=-1)
    h_fwd = mlstm_scan(qm, km, vm, ig_f, jax.nn.log_sigmoid(fg_f))
    flip = lambda t: jnp.flip(t, axis=1)
    h_bwd = flip(mlstm_scan(flip(qm), flip(km), flip(vm), flip(ig_b), flip(jax.nn.log_sigmoid(fg_b))))
    hm = h_fwd + h_bwd
    hm = hm * lax.rsqrt(jnp.mean(hm * hm, axis=-1, keepdims=True) + EPS)
    hm = hm * m_norm.astype(jnp.float32).reshape(M_HEADS, M_HEAD_DIM)
    y_m = (jax.nn.sigmoid(om.astype(jnp.float32)) * hm.reshape(B, S, M_WIDTH)).astype(h.dtype)
    return jnp.concatenate([y_att, y_m], axis=-1) @ w_out


def setup_inputs(seed: int = 0) -> dict:
    key = jax.random.key(seed)
    ks = jax.random.split(key, 24)
    f32 = jnp.float32
    nrm = lambda k, shape, scale: jax.random.normal(k, shape, f32) * scale
    gain = lambda k: 1.0 + nrm(k, (DEPTH, D_MODEL), 0.05)
    f_bias = jnp.linspace(3.0, 6.0, M_HEADS, dtype=f32)
    b_gate = jnp.concatenate([
        nrm(ks[20], (DEPTH, 2 * M_HEADS), 0.1),
        jnp.tile(f_bias, 2)[None, :] + nrm(ks[21], (DEPTH, 2 * M_HEADS), 0.1),
    ], axis=-1)
    return {
        "x": nrm(ks[0], (BATCH, SEQ, D_MODEL), 1.0),
        "ffn1_norm_pre": gain(ks[1]),
        "ffn1_norm_post": gain(ks[2]),
        "ffn1_w_gate": nrm(ks[3], (DEPTH, D_MODEL, D_FF), D_MODEL ** -0.5),
        "ffn1_w_up": nrm(ks[4], (DEPTH, D_MODEL, D_FF), D_MODEL ** -0.5),
        "ffn1_w_down": nrm(ks[5], (DEPTH, D_FF, D_MODEL), D_FF ** -0.5),
        "mix_norm_pre": gain(ks[6]),
        "mix_norm_post": gain(ks[7]),
        "w_in": nrm(ks[8], (DEPTH, D_MODEL, IN_WIDTH), D_MODEL ** -0.5),
        "b_gate": b_gate,
        "conv_w": nrm(ks[9], (DEPTH, CONV_WIDTH, 2 * M_WIDTH), CONV_WIDTH ** -0.5),
        "attn_sink": nrm(ks[10], (DEPTH, ATT_HEADS), 0.5),
        "mlstm_norm": 1.0 + nrm(ks[11], (DEPTH, M_WIDTH), 0.05),
        "w_out": nrm(ks[12], (DEPTH, MIX_WIDTH, D_MODEL), MIX_WIDTH ** -0.5),
        "ffn2_norm_pre": gain(ks[13]),
        "ffn2_norm_post": gain(ks[14]),
        "ffn2_w_gate": nrm(ks[15], (DEPTH, D_MODEL, D_FF), D_MODEL ** -0.5),
        "ffn2_w_up": nrm(ks[16], (DEPTH, D_MODEL, D_FF), D_MODEL ** -0.5),
        "ffn2_w_down": nrm(ks[17], (DEPTH, D_FF, D_MODEL), D_FF ** -0.5),
    }


def reference(x, ffn1_norm_pre, ffn1_norm_post, ffn1_w_gate, ffn1_w_up, ffn1_w_down,
              mix_norm_pre, mix_norm_post, w_in, b_gate, conv_w, attn_sink, mlstm_norm, w_out,
              ffn2_norm_pre, ffn2_norm_post, ffn2_w_gate, ffn2_w_up, ffn2_w_down):
    pos = jnp.arange(x.shape[1], dtype=jnp.float32)
    for l in range(DEPTH):
        f = swiglu(rms_norm(x, ffn1_norm_pre[l]), ffn1_w_gate[l], ffn1_w_up[l], ffn1_w_down[l])
        x = x + 0.5 * rms_norm(f, ffn1_norm_post[l])
        m = hybrid_mixer(rms_norm(x, mix_norm_pre[l]), w_in[l], b_gate[l], conv_w[l],
                         attn_sink[l], mlstm_norm[l], w_out[l], pos)
        x = x + rms_norm(m, mix_norm_post[l])
        f = swiglu(rms_norm(x, ffn2_norm_pre[l]), ffn2_w_gate[l], ffn2_w_up[l], ffn2_w_down[l])
        x = x + 0.5 * rms_norm(f, ffn2_norm_post[l])
    return x
```

```python
import functools

import jax
import jax.numpy as jnp
from jax import lax
from jax.experimental import pallas as pl
from jax.experimental.pallas import tpu as pltpu

F32 = jnp.float32
BF16 = jnp.bfloat16

EPS = 1e-6
ROPE_THETA = 10000.0
ATT_HEADS = 8
ATT_KV_HEADS = 2
ATT_GROUP = ATT_HEADS // ATT_KV_HEADS
ATT_HEAD_DIM = 128
ATT_BLOCK = 128
M_HEADS = 4
M_HEAD_DIM = 256
M_CHUNK = 128
CONV_WIDTH = 5
CONV_HALO = 16
GATE_LANES = 128
N_GATES = 4 * M_HEADS

V7X_VMEM_BYTES = 64 * 1024 * 1024
VMEM_LIMIT = 56 * 1024 * 1024


def _cparams(sem):
    return pltpu.CompilerParams(dimension_semantics=sem, vmem_limit_bytes=VMEM_LIMIT)


def _rms(x, g):
    return x * lax.rsqrt(jnp.mean(x * x, axis=-1, keepdims=True) + EPS) * g


def _ffn_body(x_ref, gpre_ref, gpost_ref, wg_ref, wu_ref, wd_ref, o_ref, h_ref, acc_ref):
    j = pl.program_id(1)
    last = pl.num_programs(1) - 1

    @pl.when(j == 0)
    def _():
        h_ref[...] = _rms(x_ref[...], gpre_ref[...]).astype(BF16)

    h = h_ref[...]
    g = jnp.dot(h, wg_ref[...], preferred_element_type=F32)
    u = jnp.dot(h, wu_ref[...], preferred_element_type=F32)
    a = (g * jax.nn.sigmoid(g) * u).astype(BF16)
    d = jnp.dot(a, wd_ref[...], preferred_element_type=F32)

    @pl.when(j == 0)
    def _():
        acc_ref[...] = d

    @pl.when(j > 0)
    def _():
        acc_ref[...] += d

    @pl.when(j == last)
    def _():
        o_ref[...] = x_ref[...] + 0.5 * _rms(acc_ref[...], gpost_ref[...])


def _ffn(x, gpre, gpost, wg, wu, wd, layer, tm=512, tf=512):
    t, d = x.shape
    f = wg.shape[-1]
    tm = min(tm, t)
    return pl.pallas_call(
        _ffn_body,
        grid=(t // tm, f // tf),
        in_specs=[
            pl.BlockSpec((tm, d), lambda i, j: (i, 0)),
            pl.BlockSpec((None, 1, d), lambda i, j: (layer, 0, 0)),
            pl.BlockSpec((None, 1, d), lambda i, j: (layer, 0, 0)),
            pl.BlockSpec((None, d, tf), lambda i, j: (layer, 0, j)),
            pl.BlockSpec((None, d, tf), lambda i, j: (layer, 0, j)),
            pl.BlockSpec((None, tf, d), lambda i, j: (layer, j, 0)),
        ],
        out_specs=pl.BlockSpec((tm, d), lambda i, j: (i, 0)),
        out_shape=jax.ShapeDtypeStruct((t, d), F32),
        scratch_shapes=[pltpu.VMEM((tm, d), BF16), pltpu.VMEM((tm, d), F32)],
        compiler_params=_cparams(("parallel", "arbitrary")),
        name="ffn",
    )(x, gpre, gpost, wg, wu, wd)


def _inproj_body(x_ref, g_ref, w_ref, wgate_ref, o_ref, og_ref, h_ref):
    j = pl.program_id(1)

    @pl.when(j == 0)
    def _():
        h_ref[...] = _rms(x_ref[...], g_ref[...]).astype(BF16)
        og_ref[...] = jnp.dot(h_ref[...], wgate_ref[...], preferred_element_type=F32)

    o_ref[...] = jnp.dot(h_ref[...], w_ref[...], preferred_element_type=F32).astype(BF16)


def _inproj(x, g, w, wgate, layer, tm=1024, tn=512):
    t, d = x.shape
    n = w.shape[-1]
    tm = min(tm, t)
    return pl.pallas_call(
        _inproj_body,
        grid=(t // tm, n // tn),
        in_specs=[
            pl.BlockSpec((tm, d), lambda i, j: (i, 0)),
            pl.BlockSpec((None, 1, d), lambda i, j: (layer, 0, 0)),
            pl.BlockSpec((None, d, tn), lambda i, j: (layer, 0, j)),
            pl.BlockSpec((None, d, GATE_LANES), lambda i, j: (layer, 0, 0)),
        ],
        out_specs=[
            pl.BlockSpec((tm, tn), lambda i, j: (i, j)),
            pl.BlockSpec((tm, GATE_LANES), lambda i, j: (i, 0)),
        ],
        out_shape=[
            jax.ShapeDtypeStruct((t, n), BF16),
            jax.ShapeDtypeStruct((t, GATE_LANES), F32),
        ],
        scratch_shapes=[pltpu.VMEM((tm, d), BF16)],
        compiler_params=_cparams(("parallel", "arbitrary")),
        name="inproj",
    )(x, g, w, wgate)


def _log_sigmoid(x):
    return -(jnp.maximum(-x, 0.0) + jnp.log1p(jnp.exp(-jnp.abs(x))))


def _gateprep_body(g_ref, bias_ref, gcol_ref, grow_ref, *, nch):
    L = M_CHUNK
    r = lax.broadcasted_iota(jnp.int32, (L, L), 0)
    c = lax.broadcasted_iota(jnp.int32, (L, L), 1)
    tril = (c <= r).astype(F32)
    triu = (c >= r).astype(F32)
    lane = lax.broadcasted_iota(jnp.int32, (L, GATE_LANES), 1)
    for ch in range(nch):
        x = g_ref[0, ch * L:(ch + 1) * L, :] + bias_ref[...]
        lf = _log_sigmoid(x)
        pre = jnp.dot(tril, lf, preferred_element_type=F32, precision=lax.Precision.HIGHEST)
        suf = jnp.dot(triu, lf, preferred_element_type=F32, precision=lax.Precision.HIGHEST)
        out = jnp.where(lane < 2 * M_HEADS, x, jnp.where(lane < 3 * M_HEADS, pre, suf))
        gcol_ref[0, ch * L:(ch + 1) * L, :] = out
        grow_ref[0, :, ch * L:(ch + 1) * L] = out.T[0:N_GATES, :]


def _gateprep(gates, bias, rows=1024):
    b, s, _ = gates.shape
    rows = min(rows, s)
    return pl.pallas_call(
        functools.partial(_gateprep_body, nch=rows // M_CHUNK),
        grid=(b, s // rows),
        in_specs=[
            pl.BlockSpec((1, rows, GATE_LANES), lambda i, j: (i, j, 0)),
            pl.BlockSpec((1, GATE_LANES), lambda i, j: (0, 0)),
        ],
        out_specs=[
            pl.BlockSpec((1, rows, GATE_LANES), lambda i, j: (i, j, 0)),
            pl.BlockSpec((1, N_GATES, rows), lambda i, j: (i, 0, j)),
        ],
        out_shape=[
            jax.ShapeDtypeStruct((b, s, GATE_LANES), F32),
            jax.ShapeDtypeStruct((b, N_GATES, s), F32),
        ],
        compiler_params=_cparams(("parallel", "parallel")),
        name="gateprep",
    )(gates, bias)


def _conv_body(xp_ref, xc_ref, xn_ref, w_ref, o_ref, buf_ref, *, ts):
    i = pl.program_id(1)
    c = pl.program_id(2)
    H = CONV_HALO
    prev = xp_ref[0].astype(F32)
    nxt = xn_ref[0].astype(F32)
    buf_ref[0:H, :] = jnp.where(i == 0, 0.0, prev)
    buf_ref[H:H + ts, :] = xc_ref[0].astype(F32)
    buf_ref[H + ts:H + ts + H, :] = jnp.where(i == pl.num_programs(1) - 1, 0.0, nxt)
    pad = CONV_WIDTH // 2
    acc = None
    for k in range(CONV_WIDTH):
        term = buf_ref[H - pad + k:H - pad + k + ts, :] * w_ref[k:k + 1, :]
        acc = term if acc is None else acc + term
    y = acc * jax.nn.sigmoid(acc)
    scale = jnp.where(c >= pl.num_programs(2) // 2, M_HEAD_DIM ** -0.5, 1.0)
    o_ref[0] = (y * scale).astype(BF16)


def _conv_silu(proj, conv_w, layer, col0, ts=512, tc=512):
    b, s, _ = proj.shape
    ts = min(ts, s)
    width = conv_w.shape[-1]
    cb0 = col0 // tc
    hb = ts // CONV_HALO
    nhb = s // CONV_HALO
    return pl.pallas_call(
        functools.partial(_conv_body, ts=ts),
        grid=(b, s // ts, width // tc),
        in_specs=[
            pl.BlockSpec((1, CONV_HALO, tc), lambda bi, i, c: (bi, jnp.maximum(i * hb - 1, 0), cb0 + c)),
            pl.BlockSpec((1, ts, tc), lambda bi, i, c: (bi, i, cb0 + c)),
            pl.BlockSpec((1, CONV_HALO, tc), lambda bi, i, c: (bi, jnp.minimum((i + 1) * hb, nhb - 1), cb0 + c)),
            pl.BlockSpec((None, CONV_WIDTH, tc), lambda bi, i, c: (layer, 0, c)),
        ],
        out_specs=pl.BlockSpec((1, ts, tc), lambda bi, i, c: (bi, i, c)),
        out_shape=jax.ShapeDtypeStruct((b, s, width), BF16),
        scratch_shapes=[pltpu.VMEM((ts + 2 * CONV_HALO, tc), F32)],
        compiler_params=_cparams(("parallel", "parallel", "parallel")),
        name="conv_silu",
    )(proj, proj, proj, conv_w)


def _rope(x_bf16, cos, sin_signed):
    n = x_bf16.shape[-1] // ATT_HEAD_DIM
    outs = []
    for h in range(n):
        x = x_bf16[:, h * ATT_HEAD_DIM:(h + 1) * ATT_HEAD_DIM].astype(F32)
        y = x * cos + pltpu.roll(x, ATT_HEAD_DIM // 2, axis=1) * sin_signed
        outs.append(y.astype(BF16))
    return outs


def _attn_body(sink_ref, q_ref, kp_ref, kc_ref, kn_ref, vp_ref, vc_ref, vn_ref,
               cp_ref, cc_ref, cn_ref, sp_ref, sc_ref, sn_ref, o_ref, *, tq, seq):
    i = pl.program_id(1)
    W = ATT_BLOCK
    nsb = tq // W
    qh = _rope(q_ref[0], cc_ref[...], sc_ref[...])
    kp = _rope(kp_ref[0], cp_ref[...], sp_ref[...])
    kc = _rope(kc_ref[0], cc_ref[...], sc_ref[...])
    kn = _rope(kn_ref[0], cn_ref[...], sn_ref[...])
    kwin = [jnp.concatenate([kp[h], kc[h], kn[h]], axis=0) for h in range(ATT_KV_HEADS)]
    vall = jnp.concatenate([vp_ref[0], vc_ref[0], vn_ref[0]], axis=0)
    r = lax.broadcasted_iota(jnp.int32, (W, 3 * W), 0)
    c = lax.broadcasted_iota(jnp.int32, (W, 3 * W), 1)
    band = jnp.abs(c - W - r) <= W
    scale = ATT_HEAD_DIM ** -0.5
    for sb in range(nsb):
        kpos = i * tq + (sb - 1) * W + c
        mask = band & (kpos >= 0) & (kpos < seq)
        for hk in range(ATT_KV_HEADS):
            q4 = jnp.concatenate(
                [qh[hk * ATT_GROUP + g][sb * W:(sb + 1) * W, :] for g in range(ATT_GROUP)], axis=0)
            kw = kwin[hk][sb * W:(sb + 3) * W, :]
            vw = vall[sb * W:(sb + 3) * W, hk * ATT_HEAD_DIM:(hk + 1) * ATT_HEAD_DIM]
            s = lax.dot_general(q4, kw, (((1,), (1,)), ((), ())), preferred_element_type=F32) * scale
            ps, dens = [], []
            for g in range(ATT_GROUP):
                sink = sink_ref[hk * ATT_GROUP + g]
                sg = jnp.where(mask, s[g * W:(g + 1) * W, :], -jnp.inf)
                m = jnp.maximum(jnp.max(sg, axis=-1, keepdims=True), sink)
                p = jnp.exp(sg - m)
                dens.append(jnp.sum(p, axis=-1, keepdims=True) + jnp.exp(sink - m))
                ps.append(p.astype(BF16))
            pv = jnp.dot(jnp.concatenate(ps, axis=0), vw, preferred_element_type=F32)
            for g in range(ATT_GROUP):
                h = hk * ATT_GROUP + g
                o_ref[0, sb * W:(sb + 1) * W, h * ATT_HEAD_DIM:(h + 1) * ATT_HEAD_DIM] = (
                    pv[g * W:(g + 1) * W, :] / dens[g]).astype(BF16)


def _attention(proj, cos2, sin2, sink, tq=512):
    b, s, _ = proj.shape
    tq = min(tq, s)
    W = ATT_BLOCK
    aw = ATT_HEADS * ATT_HEAD_DIM
    kvw = ATT_KV_HEADS * ATT_HEAD_DIM
    kb, vb = aw // kvw, aw // kvw + 1
    nb = tq // W
    nwb = s // W
    prev = lambda i: jnp.maximum(i * nb - 1, 0)
    nxt = lambda i: jnp.minimum((i + 1) * nb, nwb - 1)
    return pl.pallas_call(
        functools.partial(_attn_body, tq=tq, seq=s),
        grid=(b, s // tq),
        in_specs=[
            pl.BlockSpec(memory_space=pltpu.SMEM),
            pl.BlockSpec((1, tq, aw), lambda bi, i: (bi, i, 0)),
            pl.BlockSpec((1, W, kvw), lambda bi, i: (bi, prev(i), kb)),
            pl.BlockSpec((1, tq, kvw), lambda bi, i: (bi, i, kb)),
            pl.BlockSpec((1, W, kvw), lambda bi, i: (bi, nxt(i), kb)),
            pl.BlockSpec((1, W, kvw), lambda bi, i: (bi, prev(i), vb)),
            pl.BlockSpec((1, tq, kvw), lambda bi, i: (bi, i, vb)),
            pl.BlockSpec((1, W, kvw), lambda bi, i: (bi, nxt(i), vb)),
            pl.BlockSpec((W, ATT_HEAD_DIM), lambda bi, i: (prev(i), 0)),
            pl.BlockSpec((tq, ATT_HEAD_DIM), lambda bi, i: (i, 0)),
            pl.BlockSpec((W, ATT_HEAD_DIM), lambda bi, i: (nxt(i), 0)),
            pl.BlockSpec((W, ATT_HEAD_DIM), lambda bi, i: (prev(i), 0)),
            pl.BlockSpec((tq, ATT_HEAD_DIM), lambda bi, i: (i, 0)),
            pl.BlockSpec((W, ATT_HEAD_DIM), lambda bi, i: (nxt(i), 0)),
        ],
        out_specs=pl.BlockSpec((1, tq, aw), lambda bi, i: (bi, i, 0)),
        out_shape=jax.ShapeDtypeStruct((b, s, aw), BF16),
        compiler_params=_cparams(("parallel", "parallel")),
        name="attention",
    )(sink, proj, proj, proj, proj, proj, proj, proj, cos2, cos2, cos2, sin2, sin2, sin2)


def _mlstm_chain(q, k, v, gcol, grow, h_idx, reverse, ct_ref, n_ref, m_ref, slot):
    L = M_CHUNK
    ig_lane = (M_HEADS if reverse else 0) + h_idx
    b_lane = (3 * M_HEADS if reverse else 2 * M_HEADS) + h_idx
    igc = gcol[:, ig_lane:ig_lane + 1]
    bc = gcol[:, b_lane:b_lane + 1]
    igr = grow[ig_lane:ig_lane + 1, :]
    br = grow[b_lane:b_lane + 1, :]
    m_prev = m_ref[slot][0:1, 0:1]
    ct = ct_ref[slot]
    n = n_ref[slot]

    r = lax.broadcasted_iota(jnp.int32, (L, L), 0)
    c = lax.broadcasted_iota(jnp.int32, (L, L), 1)
    tri = (c >= r) if reverse else (c <= r)
    log_d = jnp.where(tri, bc - br + igr, -jnp.inf)
    log_inter = bc + m_prev
    m_t = jnp.maximum(log_inter, jnp.max(log_d, axis=-1, keepdims=True))
    d_mat = jnp.exp(log_d - m_t)
    inter = jnp.exp(log_inter - m_t)
    s = lax.dot_general(q, k, (((1,), (1,)), ((), ())), preferred_element_type=F32) * d_mat
    num = jnp.dot(s.astype(BF16), v, preferred_element_type=F32) + inter * jnp.dot(
        q, ct.astype(BF16), preferred_element_type=F32)
    den = jnp.sum(s, axis=-1, keepdims=True) + inter * jnp.sum(q.astype(F32) * n, axis=-1, keepdims=True)
    h = num / jnp.maximum(jnp.abs(den), jnp.exp(-m_t))

    b_last = bc[0:1, :] if reverse else bc[L - 1:L, :]
    log_w = b_last - bc + igc
    m_new = jnp.maximum(b_last + m_prev, jnp.max(log_w, axis=0, keepdims=True))
    w = jnp.exp(log_w - m_new)
    decay = jnp.exp(b_last + m_prev - m_new)
    wv = (w * v.astype(F32)).astype(BF16)
    ct_ref[slot] = decay * ct + lax.dot_general(k, wv, (((0,), (0,)), ((), ())), preferred_element_type=F32)
    n_ref[slot] = decay * n + jnp.sum(w * k.astype(F32), axis=0, keepdims=True)
    m_ref[slot] = jnp.broadcast_to(m_new, m_ref.shape[1:])
    return h


def _mlstm_body(qf_ref, kf_ref, vf0_ref, vf1_ref, gcf_ref, grf_ref,
                qb_ref, kb_ref, vb0_ref, vb1_ref, gcb_ref, grb_ref,
                hf_ref, hb_ref, ct_ref, n_ref, m_ref):
    @pl.when(pl.program_id(1) == 0)
    def _():
        ct_ref[...] = jnp.zeros_like(ct_ref)
        n_ref[...] = jnp.zeros_like(n_ref)
        m_ref[...] = jnp.zeros_like(m_ref)

    D = M_HEAD_DIM
    for reverse, (q_ref, k_ref, v_refs, gc_ref, gr_ref, h_ref) in enumerate((
            (qf_ref, kf_ref, (vf0_ref, vf1_ref), gcf_ref, grf_ref, hf_ref),
            (qb_ref, kb_ref, (vb0_ref, vb1_ref), gcb_ref, grb_ref, hb_ref))):
        gcol = gc_ref[0]
        grow = gr_ref[0]
        for h in range(M_HEADS):
            q = q_ref[0, :, h * D:(h + 1) * D]
            k = k_ref[0, :, h * D:(h + 1) * D]
            v = v_refs[h // 2][0, :, (h % 2) * D:(h % 2 + 1) * D]
            out = _mlstm_chain(q, k, v, gcol, grow, h, bool(reverse), ct_ref, n_ref, m_ref,
                               reverse * M_HEADS + h)
            h_ref[0, :, h * D:(h + 1) * D] = out


def _mlstm(qk, proj, gcol, grow, vcol0):
    b, s, _ = qk.shape
    L = M_CHUNK
    nc = s // L
    mw = M_HEADS * M_HEAD_DIM
    vw = mw // 2
    vb = vcol0 // vw
    fwd = lambda i: i
    bwd = lambda i: nc - 1 - i

    def specs(pos):
        return [
            pl.BlockSpec((1, L, mw), lambda bi, i: (bi, pos(i), 0)),
            pl.BlockSpec((1, L, mw), lambda bi, i: (bi, pos(i), 1)),
            pl.BlockSpec((1, L, vw), lambda bi, i: (bi, pos(i), vb)),
            pl.BlockSpec((1, L, vw), lambda bi, i: (bi, pos(i), vb + 1)),
            pl.BlockSpec((1, L, GATE_LANES), lambda bi, i: (bi, pos(i), 0)),
            pl.BlockSpec((1, N_GATES, L), lambda bi, i: (bi, 0, pos(i))),
        ]

    nchain = 2 * M_HEADS
    return pl.pallas_call(
        _mlstm_body,
        grid=(b, nc),
        in_specs=specs(fwd) + specs(bwd),
        out_specs=[
            pl.BlockSpec((1, L, mw), lambda bi, i: (bi, fwd(i), 0)),
            pl.BlockSpec((1, L, mw), lambda bi, i: (bi, bwd(i), 0)),
        ],
        out_shape=[jax.ShapeDtypeStruct((b, s, mw), F32), jax.ShapeDtypeStruct((b, s, mw), F32)],
        scratch_shapes=[
            pltpu.VMEM((nchain, M_HEAD_DIM, M_HEAD_DIM), F32),
            pltpu.VMEM((nchain, 1, M_HEAD_DIM), F32),
            pltpu.VMEM((nchain, 8, 128), F32),
        ],
        compiler_params=_cparams(("parallel", "arbitrary")),
        name="mlstm",
    )(qk, qk, proj, proj, gcol, grow, qk, qk, proj, proj, gcol, grow)


def _outproj_body(x_ref, ya_ref, hf_ref, hb_ref, o0_ref, o1_ref, mn_ref, w_ref, g_ref, o_ref):
    D = M_HEAD_DIM
    hm = hf_ref[...] + hb_ref[...]
    ys = []
    for h in range(M_HEADS):
        o_gate = (o0_ref, o1_ref)[h // 2][:, (h % 2) * D:(h % 2 + 1) * D].astype(F32)
        y = _rms(hm[:, h * D:(h + 1) * D], mn_ref[:, h * D:(h + 1) * D])
        ys.append((jax.nn.sigmoid(o_gate) * y).astype(BF16))
    ym = jnp.concatenate(ys, axis=-1)
    aw = ya_ref.shape[-1]
    m = jnp.dot(ya_ref[...], w_ref[0:aw, :], preferred_element_type=F32)
    m = m + jnp.dot(ym, w_ref[aw:, :], preferred_element_type=F32)
    o_ref[...] = x_ref[...] + _rms(m, g_ref[...])


def _outproj(x, ya, hf, hb, proj, mnorm, w, g, layer, ocol0, tm=256):
    t, d = x.shape
    aw = ya.shape[-1]
    mw = hf.shape[-1]
    ow = mw // 2
    ob = ocol0 // ow
    return pl.pallas_call(
        _outproj_body,
        grid=(t // tm,),
        in_specs=[
            pl.BlockSpec((tm, d), lambda i: (i, 0)),
            pl.BlockSpec((tm, aw), lambda i: (i, 0)),
            pl.BlockSpec((tm, mw), lambda i: (i, 0)),
            pl.BlockSpec((tm, mw), lambda i: (i, 0)),
            pl.BlockSpec((tm, ow), lambda i: (i, ob)),
            pl.BlockSpec((tm, ow), lambda i: (i, ob + 1)),
            pl.BlockSpec((None, 1, mw), lambda i: (layer, 0, 0)),
            pl.BlockSpec((None, aw + mw, d), lambda i: (layer, 0, 0)),
            pl.BlockSpec((None, 1, d), lambda i: (layer, 0, 0)),
        ],
        out_specs=pl.BlockSpec((tm, d), lambda i: (i, 0)),
        out_shape=jax.ShapeDtypeStruct((t, d), F32),
        compiler_params=_cparams(("parallel",)),
        name="outproj",
    )(x, ya, hf, hb, proj, proj, mnorm, w, g)


def _rope_tables(seq):
    half = ATT_HEAD_DIM // 2
    pos = jnp.arange(seq, dtype=F32)
    inv_freq = ROPE_THETA ** (-jnp.arange(half, dtype=F32) / half)
    ang = pos[:, None] * inv_freq[None, :]
    cos, sin = jnp.cos(ang), jnp.sin(ang)
    return jnp.concatenate([cos, cos], axis=-1), jnp.concatenate([-sin, sin], axis=-1)


def _mixer(x2, b, s, layer, mix_norm_pre, mix_norm_post, w_main, w_gate, b_gate_pad, conv_w,
           attn_sink, mlstm_norm, w_out, cos2, sin2):
    t, d = x2.shape
    att_w = ATT_HEADS * ATT_HEAD_DIM
    kv_w = ATT_KV_HEADS * ATT_HEAD_DIM
    m_w = M_HEADS * M_HEAD_DIM
    qm0 = att_w + 2 * kv_w
    vm0 = qm0 + 2 * m_w
    om0 = vm0 + m_w
    proj, gates = _inproj(x2, mix_norm_pre, w_main, w_gate, layer)
    proj3 = proj.reshape(b, s, proj.shape[-1])
    gcol, grow = _gateprep(gates.reshape(b, s, GATE_LANES), b_gate_pad[layer])
    qk = _conv_silu(proj3, conv_w, layer, qm0)
    ya = _attention(proj3, cos2, sin2, attn_sink[layer])
    hf, hb = _mlstm(qk, proj3, gcol, grow, vm0)
    return _outproj(x2, ya.reshape(t, att_w), hf.reshape(t, m_w), hb.reshape(t, m_w), proj,
                    mlstm_norm, w_out, mix_norm_post, layer, om0)


def kernel(x, ffn1_norm_pre, ffn1_norm_post, ffn1_w_gate, ffn1_w_up, ffn1_w_down, mix_norm_pre, mix_norm_post, w_in, b_gate, conv_w, attn_sink, mlstm_norm, w_out, ffn2_norm_pre, ffn2_norm_post, ffn2_w_gate, ffn2_w_up, ffn2_w_down):
    b, s, d = x.shape
    depth = w_in.shape[0]
    n_main = w_in.shape[-1] - N_GATES
    vec = lambda g: g.reshape(depth, 1, g.shape[-1])
    bf = lambda w: w.astype(BF16)
    w_main = bf(w_in[:, :, :n_main])
    w_gate = bf(jnp.pad(w_in[:, :, n_main:], ((0, 0), (0, 0), (0, GATE_LANES - N_GATES))))
    b_gate_pad = jnp.pad(b_gate, ((0, 0), (0, GATE_LANES - N_GATES))).reshape(depth, 1, GATE_LANES)
    f1 = (vec(ffn1_norm_pre), vec(ffn1_norm_post), bf(ffn1_w_gate), bf(ffn1_w_up), bf(ffn1_w_down))
    f2 = (vec(ffn2_norm_pre), vec(ffn2_norm_post), bf(ffn2_w_gate), bf(ffn2_w_up), bf(ffn2_w_down))
    w_out_b = bf(w_out)
    cos2, sin2 = _rope_tables(s)
    x2 = x.reshape(b * s, d)
    for layer in range(depth):
        x2 = _ffn(x2, *f1, layer)
        x2 = _mixer(x2, b, s, layer, vec(mix_norm_pre), vec(mix_norm_post), w_main, w_gate, b_gate_pad,
                    conv_w, attn_sink, vec(mlstm_norm), w_out_b, cos2, sin2)
        x2 = _ffn(x2, *f2, layer)
    return x2.reshape(b, s, d)
```

```python
import functools

import jax
import jax.numpy as jnp
from jax import lax
from jax.experimental import pallas as pl
from jax.experimental.pallas import tpu as pltpu

F32 = jnp.float32
BF16 = jnp.bfloat16

EPS = 1e-6
ROPE_THETA = 10000.0
ATT_HEADS = 8
ATT_KV_HEADS = 2
ATT_GROUP = ATT_HEADS // ATT_KV_HEADS
ATT_HEAD_DIM = 128
ATT_BLOCK = 128
M_HEADS = 4
M_HEAD_DIM = 256
M_CHUNK = 256
M_NCOL = 128
CONV_WIDTH = 5
CONV_HALO = 16
GATE_LANES = 128
N_GATES = 4 * M_HEADS

V7X_VMEM_BYTES = 64 * 1024 * 1024
VMEM_LIMIT = 56 * 1024 * 1024


def _cparams(sem):
    return pltpu.CompilerParams(dimension_semantics=sem, vmem_limit_bytes=VMEM_LIMIT)


def _rms(x, g):
    return x * lax.rsqrt(jnp.mean(x * x, axis=-1, keepdims=True) + EPS) * g


def _ffn_body(x_ref, gpre_ref, gpost_ref, wg_ref, wu_ref, wd_ref, o_ref, h_ref, acc_ref):
    j = pl.program_id(1)
    last = pl.num_programs(1) - 1

    @pl.when(j == 0)
    def _():
        h_ref[...] = _rms(x_ref[...], gpre_ref[...]).astype(BF16)
        acc_ref[...] = jnp.zeros_like(acc_ref)

    h = h_ref[...]
    g = jnp.dot(h, wg_ref[...], preferred_element_type=F32)
    u = jnp.dot(h, wu_ref[...], preferred_element_type=F32)
    a = (g * jax.nn.sigmoid(g) * u).astype(BF16)
    acc_ref[...] += jnp.dot(a, wd_ref[...], preferred_element_type=F32)

    @pl.when(j == last)
    def _():
        o_ref[...] = x_ref[...] + 0.5 * _rms(acc_ref[...], gpost_ref[...])


def _ffn(x, gpre, gpost, wg, wu, wd, layer, tm=512, tf=512):
    t, d = x.shape
    f = wg.shape[-1]
    tm = min(tm, t)
    return pl.pallas_call(
        _ffn_body,
        grid=(t // tm, f // tf),
        in_specs=[
            pl.BlockSpec((tm, d), lambda i, j: (i, 0)),
            pl.BlockSpec((None, 1, d), lambda i, j: (layer, 0, 0)),
            pl.BlockSpec((None, 1, d), lambda i, j: (layer, 0, 0)),
            pl.BlockSpec((None, d, tf), lambda i, j: (layer, 0, j)),
            pl.BlockSpec((None, d, tf), lambda i, j: (layer, 0, j)),
            pl.BlockSpec((None, tf, d), lambda i, j: (layer, j, 0)),
        ],
        out_specs=pl.BlockSpec((tm, d), lambda i, j: (i, 0)),
        out_shape=jax.ShapeDtypeStruct((t, d), F32),
        scratch_shapes=[pltpu.VMEM((tm, d), BF16), pltpu.VMEM((tm, d), F32)],
        compiler_params=_cparams(("parallel", "arbitrary")),
        name="ffn",
    )(x, gpre, gpost, wg, wu, wd)


def _inproj_body(x_ref, g_ref, w_ref, wgate_ref, o_ref, og_ref, h_ref):
    j = pl.program_id(1)

    @pl.when(j == 0)
    def _():
        h_ref[...] = _rms(x_ref[...], g_ref[...]).astype(BF16)
        og_ref[...] = jnp.dot(h_ref[...], wgate_ref[...], preferred_element_type=F32)

    o_ref[...] = jnp.dot(h_ref[...], w_ref[...], preferred_element_type=F32).astype(BF16)


def _inproj(x, g, w, wgate, layer, tm=1024, tn=512):
    t, d = x.shape
    n = w.shape[-1]
    tm = min(tm, t)
    return pl.pallas_call(
        _inproj_body,
        grid=(t // tm, n // tn),
        in_specs=[
            pl.BlockSpec((tm, d), lambda i, j: (i, 0)),
            pl.BlockSpec((None, 1, d), lambda i, j: (layer, 0, 0)),
            pl.BlockSpec((None, d, tn), lambda i, j: (layer, 0, j)),
            pl.BlockSpec((None, d, GATE_LANES), lambda i, j: (layer, 0, 0)),
        ],
        out_specs=[
            pl.BlockSpec((tm, tn), lambda i, j: (i, j)),
            pl.BlockSpec((tm, GATE_LANES), lambda i, j: (i, 0)),
        ],
        out_shape=[
            jax.ShapeDtypeStruct((t, n), BF16),
            jax.ShapeDtypeStruct((t, GATE_LANES), F32),
        ],
        scratch_shapes=[pltpu.VMEM((tm, d), BF16)],
        compiler_params=_cparams(("parallel", "arbitrary")),
        name="inproj",
    )(x, g, w, wgate)


def _log_sigmoid(x):
    return -(jnp.maximum(-x, 0.0) + jnp.log1p(jnp.exp(-jnp.abs(x))))


def _gateprep_body(g_ref, bias_ref, gcol_ref, grow_ref, *, nch):
    L = M_CHUNK
    r = lax.broadcasted_iota(jnp.int32, (L, L), 0)
    c = lax.broadcasted_iota(jnp.int32, (L, L), 1)
    tril = (c <= r).astype(F32)
    triu = (c >= r).astype(F32)
    lane = lax.broadcasted_iota(jnp.int32, (L, GATE_LANES), 1)
    for ch in range(nch):
        x = g_ref[0, ch * L:(ch + 1) * L, :] + bias_ref[...]
        lf = _log_sigmoid(x)
        pre = jnp.dot(tril, lf, preferred_element_type=F32, precision=lax.Precision.HIGHEST)
        suf = jnp.dot(triu, lf, preferred_element_type=F32, precision=lax.Precision.HIGHEST)
        out = jnp.where(lane < 2 * M_HEADS, x, jnp.where(lane < 3 * M_HEADS, pre, suf))
        gcol_ref[0, ch * L:(ch + 1) * L, :] = out
        grow_ref[0, :, ch * L:(ch + 1) * L] = out.T[0:N_GATES, :]


def _gateprep(gates, bias, rows=1024):
    b, s, _ = gates.shape
    rows = min(rows, s)
    return pl.pallas_call(
        functools.partial(_gateprep_body, nch=rows // M_CHUNK),
        grid=(b, s // rows),
        in_specs=[
            pl.BlockSpec((1, rows, GATE_LANES), lambda i, j: (i, j, 0)),
            pl.BlockSpec((1, GATE_LANES), lambda i, j: (0, 0)),
        ],
        out_specs=[
            pl.BlockSpec((1, rows, GATE_LANES), lambda i, j: (i, j, 0)),
            pl.BlockSpec((1, N_GATES, rows), lambda i, j: (i, 0, j)),
        ],
        out_shape=[
            jax.ShapeDtypeStruct((b, s, GATE_LANES), F32),
            jax.ShapeDtypeStruct((b, N_GATES, s), F32),
        ],
        compiler_params=_cparams(("parallel", "parallel")),
        name="gateprep",
    )(gates, bias)


def _conv_body(xp_ref, xc_ref, xn_ref, w_ref, o_ref, buf_ref, *, ts):
    i = pl.program_id(1)
    c = pl.program_id(2)
    H = CONV_HALO
    prev = xp_ref[0].astype(F32)
    nxt = xn_ref[0].astype(F32)
    buf_ref[0:H, :] = jnp.where(i == 0, 0.0, prev)
    buf_ref[H:H + ts, :] = xc_ref[0].astype(F32)
    buf_ref[H + ts:H + ts + H, :] = jnp.where(i == pl.num_programs(1) - 1, 0.0, nxt)
    pad = CONV_WIDTH // 2
    acc = None
    for k in range(CONV_WIDTH):
        term = buf_ref[H - pad + k:H - pad + k + ts, :] * w_ref[k:k + 1, :]
        acc = term if acc is None else acc + term
    y = acc * jax.nn.sigmoid(acc)
    scale = jnp.where(c >= pl.num_programs(2) // 2, M_HEAD_DIM ** -0.5, 1.0)
    o_ref[0] = (y * scale).astype(BF16)


def _conv_silu(proj, conv_w, layer, col0, ts=512, tc=512):
    b, s, _ = proj.shape
    ts = min(ts, s)
    width = conv_w.shape[-1]
    cb0 = col0 // tc
    hb = ts // CONV_HALO
    nhb = s // CONV_HALO
    return pl.pallas_call(
        functools.partial(_conv_body, ts=ts),
        grid=(b, s // ts, width // tc),
        in_specs=[
            pl.BlockSpec((1, CONV_HALO, tc), lambda bi, i, c: (bi, jnp.maximum(i * hb - 1, 0), cb0 + c)),
            pl.BlockSpec((1, ts, tc), lambda bi, i, c: (bi, i, cb0 + c)),
            pl.BlockSpec((1, CONV_HALO, tc), lambda bi, i, c: (bi, jnp.minimum((i + 1) * hb, nhb - 1), cb0 + c)),
            pl.BlockSpec((None, CONV_WIDTH, tc), lambda bi, i, c: (layer, 0, c)),
        ],
        out_specs=pl.BlockSpec((1, ts, tc), lambda bi, i, c: (bi, i, c)),
        out_shape=jax.ShapeDtypeStruct((b, s, width), BF16),
        scratch_shapes=[pltpu.VMEM((ts + 2 * CONV_HALO, tc), F32)],
        compiler_params=_cparams(("parallel", "parallel", "parallel")),
        name="conv_silu",
    )(proj, proj, proj, conv_w)


def _rope(x_bf16, cos, sin_signed):
    n = x_bf16.shape[-1] // ATT_HEAD_DIM
    outs = []
    for h in range(n):
        x = x_bf16[:, h * ATT_HEAD_DIM:(h + 1) * ATT_HEAD_DIM].astype(F32)
        y = x * cos + pltpu.roll(x, ATT_HEAD_DIM // 2, axis=1) * sin_signed
        outs.append(y.astype(BF16))
    return outs


def _attn_body(sink_ref, q_ref, kp_ref, kc_ref, kn_ref, vp_ref, vc_ref, vn_ref,
               cp_ref, cc_ref, cn_ref, sp_ref, sc_ref, sn_ref, o_ref, *, tq, seq):
    i = pl.program_id(1)
    W = ATT_BLOCK
    nsb = tq // W
    qh = _rope(q_ref[0], cc_ref[...], sc_ref[...])
    kp = _rope(kp_ref[0], cp_ref[...], sp_ref[...])
    kc = _rope(kc_ref[0], cc_ref[...], sc_ref[...])
    kn = _rope(kn_ref[0], cn_ref[...], sn_ref[...])
    kwin = [jnp.concatenate([kp[h], kc[h], kn[h]], axis=0) for h in range(ATT_KV_HEADS)]
    vall = jnp.concatenate([vp_ref[0], vc_ref[0], vn_ref[0]], axis=0)
    r = lax.broadcasted_iota(jnp.int32, (W, 3 * W), 0)
    c = lax.broadcasted_iota(jnp.int32, (W, 3 * W), 1)
    band = jnp.abs(c - W - r) <= W
    scale = ATT_HEAD_DIM ** -0.5
    for sb in range(nsb):
        kpos = i * tq + (sb - 1) * W + c
        mask = band & (kpos >= 0) & (kpos < seq)
        for hk in range(ATT_KV_HEADS):
            q4 = jnp.concatenate(
                [qh[hk * ATT_GROUP + g][sb * W:(sb + 1) * W, :] for g in range(ATT_GROUP)], axis=0)
            kw = kwin[hk][sb * W:(sb + 3) * W, :]
            vw = vall[sb * W:(sb + 3) * W, hk * ATT_HEAD_DIM:(hk + 1) * ATT_HEAD_DIM]
            s = lax.dot_general(q4, kw, (((1,), (1,)), ((), ())), preferred_element_type=F32) * scale
            ps, dens = [], []
            for g in range(ATT_GROUP):
                sink = sink_ref[hk * ATT_GROUP + g]
                sg = jnp.where(mask, s[g * W:(g + 1) * W, :], -jnp.inf)
                m = jnp.maximum(jnp.max(sg, axis=-1, keepdims=True), sink)
                p = jnp.exp(sg - m)
                dens.append(jnp.sum(p, axis=-1, keepdims=True) + jnp.exp(sink - m))
                ps.append(p.astype(BF16))
            pv = jnp.dot(jnp.concatenate(ps, axis=0), vw, preferred_element_type=F32)
            for g in range(ATT_GROUP):
                h = hk * ATT_GROUP + g
                o_ref[0, sb * W:(sb + 1) * W, h * ATT_HEAD_DIM:(h + 1) * ATT_HEAD_DIM] = (
                    pv[g * W:(g + 1) * W, :] / dens[g]).astype(BF16)


def _attention(proj, cos2, sin2, sink, tq=512):
    b, s, _ = proj.shape
    tq = min(tq, s)
    W = ATT_BLOCK
    aw = ATT_HEADS * ATT_HEAD_DIM
    kvw = ATT_KV_HEADS * ATT_HEAD_DIM
    kb, vb = aw // kvw, aw // kvw + 1
    nb = tq // W
    nwb = s // W
    prev = lambda i: jnp.maximum(i * nb - 1, 0)
    nxt = lambda i: jnp.minimum((i + 1) * nb, nwb - 1)
    return pl.pallas_call(
        functools.partial(_attn_body, tq=tq, seq=s),
        grid=(b, s // tq),
        in_specs=[
            pl.BlockSpec(memory_space=pltpu.SMEM),
            pl.BlockSpec((1, tq, aw), lambda bi, i: (bi, i, 0)),
            pl.BlockSpec((1, W, kvw), lambda bi, i: (bi, prev(i), kb)),
            pl.BlockSpec((1, tq, kvw), lambda bi, i: (bi, i, kb)),
            pl.BlockSpec((1, W, kvw), lambda bi, i: (bi, nxt(i), kb)),
            pl.BlockSpec((1, W, kvw), lambda bi, i: (bi, prev(i), vb)),
            pl.BlockSpec((1, tq, kvw), lambda bi, i: (bi, i, vb)),
            pl.BlockSpec((1, W, kvw), lambda bi, i: (bi, nxt(i), vb)),
            pl.BlockSpec((W, ATT_HEAD_DIM), lambda bi, i: (prev(i), 0)),
            pl.BlockSpec((tq, ATT_HEAD_DIM), lambda bi, i: (i, 0)),
            pl.BlockSpec((W, ATT_HEAD_DIM), lambda bi, i: (nxt(i), 0)),
            pl.BlockSpec((W, ATT_HEAD_DIM), lambda bi, i: (prev(i), 0)),
            pl.BlockSpec((tq, ATT_HEAD_DIM), lambda bi, i: (i, 0)),
            pl.BlockSpec((W, ATT_HEAD_DIM), lambda bi, i: (nxt(i), 0)),
        ],
        out_specs=pl.BlockSpec((1, tq, aw), lambda bi, i: (bi, i, 0)),
        out_shape=jax.ShapeDtypeStruct((b, s, aw), BF16),
        compiler_params=_cparams(("parallel", "parallel")),
        name="attention",
    )(sink, proj, proj, proj, proj, proj, proj, proj, cos2, cos2, cos2, sin2, sin2, sin2)


def _rep2(x):
    return jnp.concatenate([x, x], axis=1)


def _mlstm_body(qf_ref, kf_ref, vf0_ref, vf1_ref, gcf_ref, grf_ref,
                qb_ref, kb_ref, vb0_ref, vb1_ref, gcb_ref, grb_ref,
                hf_ref, hb_ref, st_ref, m_ref):
    @pl.when(pl.program_id(1) == 0)
    def _():
        st_ref[...] = jnp.zeros_like(st_ref)
        m_ref[...] = jnp.zeros_like(m_ref)

    D = M_HEAD_DIM
    L = qf_ref.shape[1]
    R = M_NCOL
    dirs = ((qf_ref, kf_ref, (vf0_ref, vf1_ref), gcf_ref, grf_ref, hf_ref),
            (qb_ref, kb_ref, (vb0_ref, vb1_ref), gcb_ref, grb_ref, hb_ref))
    chains = [(rev, h) for rev in range(2) for h in range(M_HEADS)]
    r = lax.broadcasted_iota(jnp.int32, (L, L), 0)
    c = lax.broadcasted_iota(jnp.int32, (L, L), 1)
    tris = (c <= r, c >= r)

    q, k, v, bc, igc, arow, m_prev = [], [], [], [], [], [], []
    for rev, h in chains:
        q_ref, k_ref, v_refs, gc_ref, gr_ref, _ = dirs[rev]
        ig_lane = rev * M_HEADS + h
        b_lane = (2 + rev) * M_HEADS + h
        q.append(q_ref[0, :, h * D:(h + 1) * D])
        k.append(k_ref[0, :, h * D:(h + 1) * D])
        v.append(v_refs[h // 2][0, :, (h % 2) * D:(h % 2 + 1) * D])
        bc.append(jnp.broadcast_to(gc_ref[0, :, b_lane:b_lane + 1], (L, R)))
        igc.append(jnp.broadcast_to(gc_ref[0, :, ig_lane:ig_lane + 1], (L, R)))
        arow.append(gr_ref[0, ig_lane:ig_lane + 1, :] - gr_ref[0, b_lane:b_lane + 1, :])
        m_prev.append(m_ref[rev * M_HEADS + h][0:1, :])

    n = len(chains)
    log_d = [jnp.where(tris[chains[i][0]], _rep2(bc[i]) + arow[i], -jnp.inf) for i in range(n)]
    row_max = [jnp.broadcast_to(jnp.max(log_d[i], axis=-1, keepdims=True), (L, R)) for i in range(n)]
    log_inter = [bc[i] + m_prev[i] for i in range(n)]
    m_t = [jnp.maximum(log_inter[i], row_max[i]) for i in range(n)]
    d_mat = [jnp.exp(log_d[i] - _rep2(m_t[i])) for i in range(n)]
    inter = [jnp.exp(log_inter[i] - m_t[i]) for i in range(n)]
    qk = [lax.dot_general(q[i], k[i], (((1,), (1,)), ((), ())), preferred_element_type=F32) for i in range(n)]
    s = [qk[i] * d_mat[i] for i in range(n)]
    p2 = [jnp.dot(q[i], st_ref[i].astype(BF16), preferred_element_type=F32) for i in range(n)]
    sv = [jnp.dot(s[i].astype(BF16), v[i], preferred_element_type=F32) for i in range(n)]
    for i, (rev, h) in enumerate(chains):
        num = sv[i] + _rep2(inter[i]) * p2[i][:, :D]
        den = jnp.broadcast_to(jnp.sum(s[i], axis=-1, keepdims=True), (L, R)) + inter[i] * p2[i][:, D:]
        scale = 1.0 / jnp.maximum(jnp.abs(den), jnp.exp(-m_t[i]))
        dirs[rev][5][0, :, h * D:(h + 1) * D] = num * _rep2(scale)

    for i, (rev, h) in enumerate(chains):
        b_last = bc[i][0:1, :] if rev else bc[i][L - 1:L, :]
        log_w = b_last - bc[i] + igc[i]
        m_new = jnp.maximum(b_last + m_prev[i], jnp.max(log_w, axis=0, keepdims=True))
        w = jnp.exp(log_w - m_new)
        decay = jnp.exp(b_last + m_prev[i] - m_new)
        wext = jnp.concatenate([(_rep2(w) * v[i].astype(F32)).astype(BF16), w.astype(BF16)], axis=1)
        upd = lax.dot_general(k[i], wext, (((0,), (0,)), ((), ())), preferred_element_type=F32)
        st_ref[i] = jnp.concatenate([decay, decay, decay], axis=1) * st_ref[i] + upd
        m_ref[i] = jnp.broadcast_to(m_new, m_ref.shape[1:])


def _mlstm(qk, proj, gcol, grow, vcol0):
    b, s, _ = qk.shape
    L = M_CHUNK
    nc = s // L
    mw = M_HEADS * M_HEAD_DIM
    vw = mw // 2
    vb = vcol0 // vw
    fwd = lambda i: i
    bwd = lambda i: nc - 1 - i

    def specs(pos):
        return [
            pl.BlockSpec((1, L, mw), lambda bi, i: (bi, pos(i), 0)),
            pl.BlockSpec((1, L, mw), lambda bi, i: (bi, pos(i), 1)),
            pl.BlockSpec((1, L, vw), lambda bi, i: (bi, pos(i), vb)),
            pl.BlockSpec((1, L, vw), lambda bi, i: (bi, pos(i), vb + 1)),
            pl.BlockSpec((1, L, GATE_LANES), lambda bi, i: (bi, pos(i), 0)),
            pl.BlockSpec((1, N_GATES, L), lambda bi, i: (bi, 0, pos(i))),
        ]

    nchain = 2 * M_HEADS
    return pl.pallas_call(
        _mlstm_body,
        grid=(b, nc),
        in_specs=specs(fwd) + specs(bwd),
        out_specs=[
            pl.BlockSpec((1, L, mw), lambda bi, i: (bi, fwd(i), 0)),
            pl.BlockSpec((1, L, mw), lambda bi, i: (bi, bwd(i), 0)),
        ],
        out_shape=[jax.ShapeDtypeStruct((b, s, mw), F32), jax.ShapeDtypeStruct((b, s, mw), F32)],
        scratch_shapes=[
            pltpu.VMEM((nchain, M_HEAD_DIM, M_HEAD_DIM + M_NCOL), F32),
            pltpu.VMEM((nchain, 8, 128), F32),
        ],
        compiler_params=_cparams(("parallel", "arbitrary")),
        name="mlstm",
    )(qk, qk, proj, proj, gcol, grow, qk, qk, proj, proj, gcol, grow)


def _outproj_body(x_ref, ya_ref, hf_ref, hb_ref, o0_ref, o1_ref, mn_ref, w_ref, g_ref, o_ref):
    D = M_HEAD_DIM
    hm = hf_ref[...] + hb_ref[...]
    ys = []
    for h in range(M_HEADS):
        o_gate = (o0_ref, o1_ref)[h // 2][:, (h % 2) * D:(h % 2 + 1) * D].astype(F32)
        y = _rms(hm[:, h * D:(h + 1) * D], mn_ref[:, h * D:(h + 1) * D])
        ys.append((jax.nn.sigmoid(o_gate) * y).astype(BF16))
    ym = jnp.concatenate(ys, axis=-1)
    aw = ya_ref.shape[-1]
    m = jnp.dot(ya_ref[...], w_ref[0:aw, :], preferred_element_type=F32)
    m = m + jnp.dot(ym, w_ref[aw:, :], preferred_element_type=F32)
    o_ref[...] = x_ref[...] + _rms(m, g_ref[...])


def _outproj(x, ya, hf, hb, proj, mnorm, w, g, layer, ocol0, tm=256):
    t, d = x.shape
    aw = ya.shape[-1]
    mw = hf.shape[-1]
    ow = mw // 2
    ob = ocol0 // ow
    return pl.pallas_call(
        _outproj_body,
        grid=(t // tm,),
        in_specs=[
            pl.BlockSpec((tm, d), lambda i: (i, 0)),
            pl.BlockSpec((tm, aw), lambda i: (i, 0)),
            pl.BlockSpec((tm, mw), lambda i: (i, 0)),
            pl.BlockSpec((tm, mw), lambda i: (i, 0)),
            pl.BlockSpec((tm, ow), lambda i: (i, ob)),
            pl.BlockSpec((tm, ow), lambda i: (i, ob + 1)),
            pl.BlockSpec((None, 1, mw), lambda i: (layer, 0, 0)),
            pl.BlockSpec((None, aw + mw, d), lambda i: (layer, 0, 0)),
            pl.BlockSpec((None, 1, d), lambda i: (layer, 0, 0)),
        ],
        out_specs=pl.BlockSpec((tm, d), lambda i: (i, 0)),
        out_shape=jax.ShapeDtypeStruct((t, d), F32),
        compiler_params=_cparams(("parallel",)),
        name="outproj",
    )(x, ya, hf, hb, proj, proj, mnorm, w, g)


def _rope_tables(seq):
    half = ATT_HEAD_DIM // 2
    pos = jnp.arange(seq, dtype=F32)
    inv_freq = ROPE_THETA ** (-jnp.arange(half, dtype=F32) / half)
    ang = pos[:, None] * inv_freq[None, :]
    cos, sin = jnp.cos(ang), jnp.sin(ang)
    return jnp.concatenate([cos, cos], axis=-1), jnp.concatenate([-sin, sin], axis=-1)


def _mixer(x2, b, s, layer, mix_norm_pre, mix_norm_post, w_main, w_gate, b_gate_pad, conv_w,
           attn_sink, mlstm_norm, w_out, cos2, sin2):
    t, d = x2.shape
    att_w = ATT_HEADS * ATT_HEAD_DIM
    kv_w = ATT_KV_HEADS * ATT_HEAD_DIM
    m_w = M_HEADS * M_HEAD_DIM
    qm0 = att_w + 2 * kv_w
    vm0 = qm0 + 2 * m_w
    om0 = vm0 + m_w
    proj, gates = _inproj(x2, mix_norm_pre, w_main, w_gate, layer)
    proj3 = proj.reshape(b, s, proj.shape[-1])
    gcol, grow = _gateprep(gates.reshape(b, s, GATE_LANES), b_gate_pad[layer])
    qk = _conv_silu(proj3, conv_w, layer, qm0)
    ya = _attention(proj3, cos2, sin2, attn_sink[layer])
    hf, hb = _mlstm(qk, proj3, gcol, grow, vm0)
    return _outproj(x2, ya.reshape(t, att_w), hf.reshape(t, m_w), hb.reshape(t, m_w), proj,
                    mlstm_norm, w_out, mix_norm_post, layer, om0)


def kernel(x, ffn1_norm_pre, ffn1_norm_post, ffn1_w_gate, ffn1_w_up, ffn1_w_down, mix_norm_pre, mix_norm_post, w_in, b_gate, conv_w, attn_sink, mlstm_norm, w_out, ffn2_norm_pre, ffn2_norm_post, ffn2_w_gate, ffn2_w_up, ffn2_w_down):
    b, s, d = x.shape
    depth = w_in.shape[0]
    n_main = w_in.shape[-1] - N_GATES
    vec = lambda g: g.reshape(depth, 1, g.shape[-1])
    bf = lambda w: w.astype(BF16)
    w_main = bf(w_in[:, :, :n_main])
    w_gate = bf(jnp.pad(w_in[:, :, n_main:], ((0, 0), (0, 0), (0, GATE_LANES - N_GATES))))
    b_gate_pad = jnp.pad(b_gate, ((0, 0), (0, GATE_LANES - N_GATES))).reshape(depth, 1, GATE_LANES)
    f1 = (vec(ffn1_norm_pre), vec(ffn1_norm_post), bf(ffn1_w_gate), bf(ffn1_w_up), bf(ffn1_w_down))
    f2 = (vec(ffn2_norm_pre), vec(ffn2_norm_post), bf(ffn2_w_gate), bf(ffn2_w_up), bf(ffn2_w_down))
    w_out_b = bf(w_out)
    cos2, sin2 = _rope_tables(s)
    x2 = x.reshape(b * s, d)
    for layer in range(depth):
        x2 = _ffn(x2, *f1, layer)
        x2 = _mixer(x2, b, s, layer, vec(mix_norm_pre), vec(mix_norm_post), w_main, w_gate, b_gate_pad,
                    conv_w, attn_sink, vec(mlstm_norm), w_out_b, cos2, sin2)
        x2 = _ffn(x2, *f2, layer)
    return x2.reshape(b, s, d)
```

```python
import functools

import jax
import jax.numpy as jnp
from jax import lax
from jax.experimental import pallas as pl
from jax.experimental.pallas import tpu as pltpu

F32 = jnp.float32
BF16 = jnp.bfloat16

EPS = 1e-6
ROPE_THETA = 10000.0
ATT_HEADS = 8
ATT_KV_HEADS = 2
ATT_GROUP = ATT_HEADS // ATT_KV_HEADS
ATT_HEAD_DIM = 128
ATT_BLOCK = 128
M_HEADS = 4
M_HEAD_DIM = 256
M_CHUNK = 256
M_NCOL = 128
CONV_WIDTH = 5
CONV_HALO = 16
GATE_LANES = 128
N_GATES = 4 * M_HEADS

V7X_VMEM_BYTES = 64 * 1024 * 1024
VMEM_LIMIT = 56 * 1024 * 1024


def _cparams(sem):
    return pltpu.CompilerParams(dimension_semantics=sem, vmem_limit_bytes=VMEM_LIMIT)


def _rms(x, g):
    return x * lax.rsqrt(jnp.mean(x * x, axis=-1, keepdims=True) + EPS) * g


def _ffn_body(xe_ref, xn_ref, gpre_ref, gpost_ref, wg_ref, wu_ref, wd_ref, o_ref, h_ref, acc_ref, *, n_chunks):
    r, j = pl.program_id(0), pl.program_id(1)
    last_r = pl.num_programs(0) - 1
    par = r % 2
    oth = 1 - par
    rc = xn_ref.shape[0]
    rows = pl.ds(pl.multiple_of(jnp.minimum(j, n_chunks - 1) * rc, rc), rc)
    g_pre = gpre_ref[...]
    g_post_half = 0.5 * gpost_ref[...]

    def prenorm_chunk():
        return _rms(xn_ref[...], g_pre).astype(BF16)

    def postnorm_chunk():
        return xe_ref[...] + _rms(acc_ref[par, rows, :], g_post_half)

    @pl.when(r == 0)
    def _():
        @pl.when(j == 0)
        def _():
            acc_ref[...] = jnp.zeros_like(acc_ref)
        h_ref[par, rows, :] = prenorm_chunk()

    @pl.when((r > 0) & (r < last_r))
    def _():
        out_chunk = postnorm_chunk()
        h_chunk = prenorm_chunk()
        h = h_ref[oth]
        g = jnp.dot(h, wg_ref[...], preferred_element_type=F32)
        u = jnp.dot(h, wu_ref[...], preferred_element_type=F32)
        a = (g * jax.nn.sigmoid(g) * u).astype(BF16)
        acc_ref[oth] = jnp.where(j == 0, 0.0, acc_ref[oth]) + jnp.dot(a, wd_ref[...], preferred_element_type=F32)
        o_ref[...] = out_chunk
        h_ref[par, rows, :] = h_chunk

    @pl.when(r == last_r)
    def _():
        o_ref[...] = postnorm_chunk()


def _ffn(x, gpre, gpost, wg, wu, wd, layer, tm=1024, tf=512, rows_per_step=128):
    t, d = x.shape
    f = wg.shape[-1]
    tm = min(tm, t)
    nt, nj = t // tm, f // tf
    n_chunks = tm // rows_per_step
    assert n_chunks <= nj, "every row chunk of a tile needs its own d_ff step"
    n_blocks = nt * n_chunks
    chunk = lambda j: jnp.minimum(j, n_chunks - 1)
    emit_map = lambda r, j: (jnp.clip((r - 2) * n_chunks + chunk(j), 0, n_blocks - 1), 0)
    next_map = lambda r, j: (jnp.minimum(r * n_chunks + chunk(j), n_blocks - 1), 0)
    wj = lambda r, j: jnp.where(r == 0, 0, jnp.where(r == nt + 1, nj - 1, j))

    return pl.pallas_call(
        functools.partial(_ffn_body, n_chunks=n_chunks),
        grid=(nt + 2, nj),
        in_specs=[
            pl.BlockSpec((rows_per_step, d), emit_map),
            pl.BlockSpec((rows_per_step, d), next_map),
            pl.BlockSpec((None, 1, d), lambda r, j: (layer, 0, 0)),
            pl.BlockSpec((None, 1, d), lambda r, j: (layer, 0, 0)),
            pl.BlockSpec((None, d, tf), lambda r, j: (layer, 0, wj(r, j))),
            pl.BlockSpec((None, d, tf), lambda r, j: (layer, 0, wj(r, j))),
            pl.BlockSpec((None, tf, d), lambda r, j: (layer, wj(r, j), 0)),
        ],
        out_specs=pl.BlockSpec((rows_per_step, d), emit_map),
        out_shape=jax.ShapeDtypeStruct((t, d), F32),
        scratch_shapes=[pltpu.VMEM((2, tm, d), BF16), pltpu.VMEM((2, tm, d), F32)],
        compiler_params=_cparams(("arbitrary", "arbitrary")),
        name="ffn",
    )(x, x, gpre, gpost, wg, wu, wd)


def _inproj_body(x_ref, g_ref, w_ref, wgate_ref, o_ref, og_ref, h_ref):
    r = pl.program_id(0)
    j = pl.program_id(1)
    par = r % 2
    rc = x_ref.shape[0]
    rows = pl.ds(pl.multiple_of(j * rc, rc), rc)
    h_chunk = _rms(x_ref[...], g_ref[...]).astype(BF16)
    og_ref[...] = jnp.dot(h_chunk, wgate_ref[...], preferred_element_type=F32)

    @pl.when(r > 0)
    def _():
        o_ref[...] = jnp.dot(h_ref[1 - par], w_ref[...], preferred_element_type=F32).astype(BF16)

    h_ref[par, rows, :] = h_chunk


def _inproj(x, g, w, wgate, layer, tm=512, tn=2816):
    t, d = x.shape
    n = w.shape[-1]
    tm = min(tm, t)
    nt, nj = t // tm, n // tn
    rc = tm // nj
    chunk_map = lambda r, j: (jnp.minimum(r * nj + j, nt * nj - 1), 0)
    return pl.pallas_call(
        _inproj_body,
        grid=(nt + 1, nj),
        in_specs=[
            pl.BlockSpec((rc, d), chunk_map),
            pl.BlockSpec((None, 1, d), lambda r, j: (layer, 0, 0)),
            pl.BlockSpec((None, d, tn), lambda r, j: (layer, 0, j)),
            pl.BlockSpec((None, d, GATE_LANES), lambda r, j: (layer, 0, 0)),
        ],
        out_specs=[
            pl.BlockSpec((tm, tn), lambda r, j: (jnp.maximum(r - 1, 0), jnp.where(r == 0, 0, j))),
            pl.BlockSpec((rc, GATE_LANES), chunk_map),
        ],
        out_shape=[
            jax.ShapeDtypeStruct((t, n), BF16),
            jax.ShapeDtypeStruct((t, GATE_LANES), F32),
        ],
        scratch_shapes=[pltpu.VMEM((2, tm, d), BF16)],
        compiler_params=_cparams(("arbitrary", "arbitrary")),
        name="inproj",
    )(x, g, w, wgate)


def _log_sigmoid(x):
    return -(jnp.maximum(-x, 0.0) + jnp.log1p(jnp.exp(-jnp.abs(x))))


def _gateprep_body(g_ref, bias_ref, gcol_ref, grow_ref, *, nch):
    L = M_CHUNK
    r = lax.broadcasted_iota(jnp.int32, (L, L), 0)
    c = lax.broadcasted_iota(jnp.int32, (L, L), 1)
    tril = (c <= r).astype(F32)
    triu = (c >= r).astype(F32)
    lane = lax.broadcasted_iota(jnp.int32, (L, GATE_LANES), 1)
    for ch in range(nch):
        x = g_ref[0, ch * L:(ch + 1) * L, :] + bias_ref[...]
        lf = _log_sigmoid(x)
        pre = jnp.dot(tril, lf, preferred_element_type=F32, precision=lax.Precision.HIGHEST)
        suf = jnp.dot(triu, lf, preferred_element_type=F32, precision=lax.Precision.HIGHEST)
        out = jnp.where(lane < 2 * M_HEADS, x, jnp.where(lane < 3 * M_HEADS, pre, suf))
        gcol_ref[0, ch * L:(ch + 1) * L, :] = out
        grow_ref[0, :, ch * L:(ch + 1) * L] = out.T[0:N_GATES, :]


def _gateprep(gates, bias, rows=1024):
    b, s, _ = gates.shape
    rows = min(rows, s)
    return pl.pallas_call(
        functools.partial(_gateprep_body, nch=rows // M_CHUNK),
        grid=(b, s // rows),
        in_specs=[
            pl.BlockSpec((1, rows, GATE_LANES), lambda i, j: (i, j, 0)),
            pl.BlockSpec((1, GATE_LANES), lambda i, j: (0, 0)),
        ],
        out_specs=[
            pl.BlockSpec((1, rows, GATE_LANES), lambda i, j: (i, j, 0)),
            pl.BlockSpec((1, N_GATES, rows), lambda i, j: (i, 0, j)),
        ],
        out_shape=[
            jax.ShapeDtypeStruct((b, s, GATE_LANES), F32),
            jax.ShapeDtypeStruct((b, N_GATES, s), F32),
        ],
        compiler_params=_cparams(("parallel", "parallel")),
        name="gateprep",
    )(gates, bias)


def _conv_body(xp_ref, xc_ref, xn_ref, w_ref, o_ref, buf_ref, *, ts):
    i = pl.program_id(1)
    c = pl.program_id(2)
    H = CONV_HALO
    prev = xp_ref[0].astype(F32)
    nxt = xn_ref[0].astype(F32)
    buf_ref[0:H, :] = jnp.where(i == 0, 0.0, prev)
    buf_ref[H:H + ts, :] = xc_ref[0].astype(F32)
    buf_ref[H + ts:H + ts + H, :] = jnp.where(i == pl.num_programs(1) - 1, 0.0, nxt)
    pad = CONV_WIDTH // 2
    acc = None
    for k in range(CONV_WIDTH):
        term = buf_ref[H - pad + k:H - pad + k + ts, :] * w_ref[k:k + 1, :]
        acc = term if acc is None else acc + term
    y = acc * jax.nn.sigmoid(acc)
    scale = jnp.where(c >= pl.num_programs(2) // 2, M_HEAD_DIM ** -0.5, 1.0)
    o_ref[0] = (y * scale).astype(BF16)


def _conv_silu(proj, conv_w, layer, col0, ts=512, tc=512):
    b, s, _ = proj.shape
    ts = min(ts, s)
    width = conv_w.shape[-1]
    cb0 = col0 // tc
    hb = ts // CONV_HALO
    nhb = s // CONV_HALO
    return pl.pallas_call(
        functools.partial(_conv_body, ts=ts),
        grid=(b, s // ts, width // tc),
        in_specs=[
            pl.BlockSpec((1, CONV_HALO, tc), lambda bi, i, c: (bi, jnp.maximum(i * hb - 1, 0), cb0 + c)),
            pl.BlockSpec((1, ts, tc), lambda bi, i, c: (bi, i, cb0 + c)),
            pl.BlockSpec((1, CONV_HALO, tc), lambda bi, i, c: (bi, jnp.minimum((i + 1) * hb, nhb - 1), cb0 + c)),
            pl.BlockSpec((None, CONV_WIDTH, tc), lambda bi, i, c: (layer, 0, c)),
        ],
        out_specs=pl.BlockSpec((1, ts, tc), lambda bi, i, c: (bi, i, c)),
        out_shape=jax.ShapeDtypeStruct((b, s, width), BF16),
        scratch_shapes=[pltpu.VMEM((ts + 2 * CONV_HALO, tc), F32)],
        compiler_params=_cparams(("parallel", "parallel", "parallel")),
        name="conv_silu",
    )(proj, proj, proj, conv_w)


def _rope(x_bf16, cos, sin_signed):
    n = x_bf16.shape[-1] // ATT_HEAD_DIM
    outs = []
    for h in range(n):
        x = x_bf16[:, h * ATT_HEAD_DIM:(h + 1) * ATT_HEAD_DIM].astype(F32)
        y = x * cos + pltpu.roll(x, ATT_HEAD_DIM // 2, axis=1) * sin_signed
        outs.append(y.astype(BF16))
    return outs


def _attn_body(sink_ref, q_ref, kp_ref, kc_ref, kn_ref, vp_ref, vc_ref, vn_ref,
               cp_ref, cc_ref, cn_ref, sp_ref, sc_ref, sn_ref, o_ref, *, tq, seq):
    i = pl.program_id(1)
    W = ATT_BLOCK
    nsb = tq // W
    qh = _rope(q_ref[0], cc_ref[...], sc_ref[...])
    kp = _rope(kp_ref[0], cp_ref[...], sp_ref[...])
    kc = _rope(kc_ref[0], cc_ref[...], sc_ref[...])
    kn = _rope(kn_ref[0], cn_ref[...], sn_ref[...])
    kwin = [jnp.concatenate([kp[h], kc[h], kn[h]], axis=0) for h in range(ATT_KV_HEADS)]
    vall = jnp.concatenate([vp_ref[0], vc_ref[0], vn_ref[0]], axis=0)
    r = lax.broadcasted_iota(jnp.int32, (W, 3 * W), 0)
    c = lax.broadcasted_iota(jnp.int32, (W, 3 * W), 1)
    band = jnp.abs(c - W - r) <= W
    scale = ATT_HEAD_DIM ** -0.5
    for sb in range(nsb):
        kpos = i * tq + (sb - 1) * W + c
        mask = band & (kpos >= 0) & (kpos < seq)
        for hk in range(ATT_KV_HEADS):
            q4 = jnp.concatenate(
                [qh[hk * ATT_GROUP + g][sb * W:(sb + 1) * W, :] for g in range(ATT_GROUP)], axis=0)
            kw = kwin[hk][sb * W:(sb + 3) * W, :]
            vw = vall[sb * W:(sb + 3) * W, hk * ATT_HEAD_DIM:(hk + 1) * ATT_HEAD_DIM]
            s = lax.dot_general(q4, kw, (((1,), (1,)), ((), ())), preferred_element_type=F32) * scale
            ps, dens = [], []
            for g in range(ATT_GROUP):
                sink = sink_ref[hk * ATT_GROUP + g]
                sg = jnp.where(mask, s[g * W:(g + 1) * W, :], -jnp.inf)
                m = jnp.maximum(jnp.max(sg, axis=-1, keepdims=True), sink)
                p = jnp.exp(sg - m)
                dens.append(jnp.sum(p, axis=-1, keepdims=True) + jnp.exp(sink - m))
                ps.append(p.astype(BF16))
            pv = jnp.dot(jnp.concatenate(ps, axis=0), vw, preferred_element_type=F32)
            for g in range(ATT_GROUP):
                h = hk * ATT_GROUP + g
                o_ref[0, sb * W:(sb + 1) * W, h * ATT_HEAD_DIM:(h + 1) * ATT_HEAD_DIM] = (
                    pv[g * W:(g + 1) * W, :] / dens[g]).astype(BF16)


def _attention(proj, cos2, sin2, sink, tq=512):
    b, s, _ = proj.shape
    tq = min(tq, s)
    W = ATT_BLOCK
    aw = ATT_HEADS * ATT_HEAD_DIM
    kvw = ATT_KV_HEADS * ATT_HEAD_DIM
    kb, vb = aw // kvw, aw // kvw + 1
    nb = tq // W
    nwb = s // W
    prev = lambda i: jnp.maximum(i * nb - 1, 0)
    nxt = lambda i: jnp.minimum((i + 1) * nb, nwb - 1)
    return pl.pallas_call(
        functools.partial(_attn_body, tq=tq, seq=s),
        grid=(b, s // tq),
        in_specs=[
            pl.BlockSpec(memory_space=pltpu.SMEM),
            pl.BlockSpec((1, tq, aw), lambda bi, i: (bi, i, 0)),
            pl.BlockSpec((1, W, kvw), lambda bi, i: (bi, prev(i), kb)),
            pl.BlockSpec((1, tq, kvw), lambda bi, i: (bi, i, kb)),
            pl.BlockSpec((1, W, kvw), lambda bi, i: (bi, nxt(i), kb)),
            pl.BlockSpec((1, W, kvw), lambda bi, i: (bi, prev(i), vb)),
            pl.BlockSpec((1, tq, kvw), lambda bi, i: (bi, i, vb)),
            pl.BlockSpec((1, W, kvw), lambda bi, i: (bi, nxt(i), vb)),
            pl.BlockSpec((W, ATT_HEAD_DIM), lambda bi, i: (prev(i), 0)),
            pl.BlockSpec((tq, ATT_HEAD_DIM), lambda bi, i: (i, 0)),
            pl.BlockSpec((W, ATT_HEAD_DIM), lambda bi, i: (nxt(i), 0)),
            pl.BlockSpec((W, ATT_HEAD_DIM), lambda bi, i: (prev(i), 0)),
            pl.BlockSpec((tq, ATT_HEAD_DIM), lambda bi, i: (i, 0)),
            pl.BlockSpec((W, ATT_HEAD_DIM), lambda bi, i: (nxt(i), 0)),
        ],
        out_specs=pl.BlockSpec((1, tq, aw), lambda bi, i: (bi, i, 0)),
        out_shape=jax.ShapeDtypeStruct((b, s, aw), BF16),
        compiler_params=_cparams(("parallel", "parallel")),
        name="attention",
    )(sink, proj, proj, proj, proj, proj, proj, proj, cos2, cos2, cos2, sin2, sin2, sin2)


def _rep2(x):
    return jnp.concatenate([x, x], axis=1)


def _mlstm_body(qf_ref, kf_ref, vf0_ref, vf1_ref, gcf_ref, grf_ref,
                qb_ref, kb_ref, vb0_ref, vb1_ref, gcb_ref, grb_ref,
                hf_ref, hb_ref, st_ref, m_ref):
    @pl.when(pl.program_id(1) == 0)
    def _():
        st_ref[...] = jnp.zeros_like(st_ref)
        m_ref[...] = jnp.zeros_like(m_ref)

    D = M_HEAD_DIM
    L = qf_ref.shape[1]
    R = M_NCOL
    dirs = ((qf_ref, kf_ref, (vf0_ref, vf1_ref), gcf_ref, grf_ref, hf_ref),
            (qb_ref, kb_ref, (vb0_ref, vb1_ref), gcb_ref, grb_ref, hb_ref))
    chains = [(rev, h) for rev in range(2) for h in range(M_HEADS)]
    r = lax.broadcasted_iota(jnp.int32, (L, L), 0)
    c = lax.broadcasted_iota(jnp.int32, (L, L), 1)
    tris = (c <= r, c >= r)

    q, k, v, bc, igc, arow, m_prev = [], [], [], [], [], [], []
    for rev, h in chains:
        q_ref, k_ref, v_refs, gc_ref, gr_ref, _ = dirs[rev]
        ig_lane = rev * M_HEADS + h
        b_lane = (2 + rev) * M_HEADS + h
        q.append(q_ref[0, :, h * D:(h + 1) * D])
        k.append(k_ref[0, :, h * D:(h + 1) * D])
        v.append(v_refs[h // 2][0, :, (h % 2) * D:(h % 2 + 1) * D])
        bc.append(jnp.broadcast_to(gc_ref[0, :, b_lane:b_lane + 1], (L, R)))
        igc.append(jnp.broadcast_to(gc_ref[0, :, ig_lane:ig_lane + 1], (L, R)))
        arow.append(gr_ref[0, ig_lane:ig_lane + 1, :] - gr_ref[0, b_lane:b_lane + 1, :])
        m_prev.append(m_ref[rev * M_HEADS + h][0:1, :])

    n = len(chains)
    log_d = [jnp.where(tris[chains[i][0]], _rep2(bc[i]) + arow[i], -jnp.inf) for i in range(n)]
    row_max = [jnp.broadcast_to(jnp.max(log_d[i], axis=-1, keepdims=True), (L, R)) for i in range(n)]
    log_inter = [bc[i] + m_prev[i] for i in range(n)]
    m_t = [jnp.maximum(log_inter[i], row_max[i]) for i in range(n)]
    d_mat = [jnp.exp(log_d[i] - _rep2(m_t[i])) for i in range(n)]
    inter = [jnp.exp(log_inter[i] - m_t[i]) for i in range(n)]
    qk = [lax.dot_general(q[i], k[i], (((1,), (1,)), ((), ())), preferred_element_type=F32) for i in range(n)]
    s = [qk[i] * d_mat[i] for i in range(n)]
    p2 = [jnp.dot(q[i], st_ref[i].astype(BF16), preferred_element_type=F32) for i in range(n)]
    sv = [jnp.dot(s[i].astype(BF16), v[i], preferred_element_type=F32) for i in range(n)]
    for i, (rev, h) in enumerate(chains):
        num = sv[i] + _rep2(inter[i]) * p2[i][:, :D]
        den = jnp.broadcast_to(jnp.sum(s[i], axis=-1, keepdims=True), (L, R)) + inter[i] * p2[i][:, D:]
        scale = 1.0 / jnp.maximum(jnp.abs(den), jnp.exp(-m_t[i]))
        dirs[rev][5][0, :, h * D:(h + 1) * D] = num * _rep2(scale)

    for i, (rev, h) in enumerate(chains):
        b_last = bc[i][0:1, :] if rev else bc[i][L - 1:L, :]
        log_w = b_last - bc[i] + igc[i]
        m_new = jnp.maximum(b_last + m_prev[i], jnp.max(log_w, axis=0, keepdims=True))
        w = jnp.exp(log_w - m_new)
        decay = jnp.exp(b_last + m_prev[i] - m_new)
        wext = jnp.concatenate([(_rep2(w) * v[i].astype(F32)).astype(BF16), w.astype(BF16)], axis=1)
        upd = lax.dot_general(k[i], wext, (((0,), (0,)), ((), ())), preferred_element_type=F32)
        st_ref[i] = jnp.concatenate([decay, decay, decay], axis=1) * st_ref[i] + upd
        m_ref[i] = jnp.broadcast_to(m_new, m_ref.shape[1:])


def _mlstm(qk, proj, gcol, grow, vcol0):
    b, s, _ = qk.shape
    L = M_CHUNK
    nc = s // L
    mw = M_HEADS * M_HEAD_DIM
    vw = mw // 2
    vb = vcol0 // vw
    fwd = lambda i: i
    bwd = lambda i: nc - 1 - i

    def specs(pos):
        return [
            pl.BlockSpec((1, L, mw), lambda bi, i: (bi, pos(i), 0)),
            pl.BlockSpec((1, L, mw), lambda bi, i: (bi, pos(i), 1)),
            pl.BlockSpec((1, L, vw), lambda bi, i: (bi, pos(i), vb)),
            pl.BlockSpec((1, L, vw), lambda bi, i: (bi, pos(i), vb + 1)),
            pl.BlockSpec((1, L, GATE_LANES), lambda bi, i: (bi, pos(i), 0)),
            pl.BlockSpec((1, N_GATES, L), lambda bi, i: (bi, 0, pos(i))),
        ]

    nchain = 2 * M_HEADS
    return pl.pallas_call(
        _mlstm_body,
        grid=(b, nc),
        in_specs=specs(fwd) + specs(bwd),
        out_specs=[
            pl.BlockSpec((1, L, mw), lambda bi, i: (bi, fwd(i), 0)),
            pl.BlockSpec((1, L, mw), lambda bi, i: (bi, bwd(i), 0)),
        ],
        out_shape=[jax.ShapeDtypeStruct((b, s, mw), F32), jax.ShapeDtypeStruct((b, s, mw), F32)],
        scratch_shapes=[
            pltpu.VMEM((nchain, M_HEAD_DIM, M_HEAD_DIM + M_NCOL), F32),
            pltpu.VMEM((nchain, 8, 128), F32),
        ],
        compiler_params=_cparams(("parallel", "arbitrary")),
        name="mlstm",
    )(qk, qk, proj, proj, gcol, grow, qk, qk, proj, proj, gcol, grow)


def _outproj_body(x_ref, ya_ref, hf_ref, hb_ref, o0_ref, o1_ref, mn_ref, w_ref, g_ref, o_ref):
    D = M_HEAD_DIM
    hm = hf_ref[...] + hb_ref[...]
    ys = []
    for h in range(M_HEADS):
        o_gate = (o0_ref, o1_ref)[h // 2][:, (h % 2) * D:(h % 2 + 1) * D].astype(F32)
        y = _rms(hm[:, h * D:(h + 1) * D], mn_ref[:, h * D:(h + 1) * D])
        ys.append((jax.nn.sigmoid(o_gate) * y).astype(BF16))
    ym = jnp.concatenate(ys, axis=-1)
    aw = ya_ref.shape[-1]
    m = jnp.dot(ya_ref[...], w_ref[0:aw, :], preferred_element_type=F32)
    m = m + jnp.dot(ym, w_ref[aw:, :], preferred_element_type=F32)
    o_ref[...] = x_ref[...] + _rms(m, g_ref[...])


def _outproj(x, ya, hf, hb, proj, mnorm, w, g, layer, ocol0, tm=512):
    t, d = x.shape
    aw = ya.shape[-1]
    mw = hf.shape[-1]
    ow = mw // 2
    ob = ocol0 // ow
    return pl.pallas_call(
        _outproj_body,
        grid=(t // tm,),
        in_specs=[
            pl.BlockSpec((tm, d), lambda i: (i, 0)),
            pl.BlockSpec((tm, aw), lambda i: (i, 0)),
            pl.BlockSpec((tm, mw), lambda i: (i, 0)),
            pl.BlockSpec((tm, mw), lambda i: (i, 0)),
            pl.BlockSpec((tm, ow), lambda i: (i, ob)),
            pl.BlockSpec((tm, ow), lambda i: (i, ob + 1)),
            pl.BlockSpec((None, 1, mw), lambda i: (layer, 0, 0)),
            pl.BlockSpec((None, aw + mw, d), lambda i: (layer, 0, 0)),
            pl.BlockSpec((None, 1, d), lambda i: (layer, 0, 0)),
        ],
        out_specs=pl.BlockSpec((tm, d), lambda i: (i, 0)),
        out_shape=jax.ShapeDtypeStruct((t, d), F32),
        compiler_params=_cparams(("parallel",)),
        name="outproj",
    )(x, ya, hf, hb, proj, proj, mnorm, w, g)


def _rope_tables(seq):
    half = ATT_HEAD_DIM // 2
    pos = jnp.arange(seq, dtype=F32)
    inv_freq = ROPE_THETA ** (-jnp.arange(half, dtype=F32) / half)
    ang = pos[:, None] * inv_freq[None, :]
    cos, sin = jnp.cos(ang), jnp.sin(ang)
    return jnp.concatenate([cos, cos], axis=-1), jnp.concatenate([-sin, sin], axis=-1)


def _mixer(x2, b, s, layer, mix_norm_pre, mix_norm_post, w_main, w_gate, b_gate_pad, conv_w,
           attn_sink, mlstm_norm, w_out, cos2, sin2):
    t, d = x2.shape
    att_w = ATT_HEADS * ATT_HEAD_DIM
    kv_w = ATT_KV_HEADS * ATT_HEAD_DIM
    m_w = M_HEADS * M_HEAD_DIM
    qm0 = att_w + 2 * kv_w
    vm0 = qm0 + 2 * m_w
    om0 = vm0 + m_w
    proj, gates = _inproj(x2, mix_norm_pre, w_main, w_gate, layer)
    proj3 = proj.reshape(b, s, proj.shape[-1])
    gcol, grow = _gateprep(gates.reshape(b, s, GATE_LANES), b_gate_pad[layer])
    qk = _conv_silu(proj3, conv_w, layer, qm0)
    ya = _attention(proj3, cos2, sin2, attn_sink[layer])
    hf, hb = _mlstm(qk, proj3, gcol, grow, vm0)
    return _outproj(x2, ya.reshape(t, att_w), hf.reshape(t, m_w), hb.reshape(t, m_w), proj,
                    mlstm_norm, w_out, mix_norm_post, layer, om0)


def kernel(x, ffn1_norm_pre, ffn1_norm_post, ffn1_w_gate, ffn1_w_up, ffn1_w_down, mix_norm_pre, mix_norm_post, w_in, b_gate, conv_w, attn_sink, mlstm_norm, w_out, ffn2_norm_pre, ffn2_norm_post, ffn2_w_gate, ffn2_w_up, ffn2_w_down):
    b, s, d = x.shape
    depth = w_in.shape[0]
    n_main = w_in.shape[-1] - N_GATES
    vec = lambda g: g.reshape(depth, 1, g.shape[-1])
    bf = lambda w: w.astype(BF16)
    w_main = bf(w_in[:, :, :n_main])
    w_gate = bf(jnp.pad(w_in[:, :, n_main:], ((0, 0), (0, 0), (0, GATE_LANES - N_GATES))))
    b_gate_pad = jnp.pad(b_gate, ((0, 0), (0, GATE_LANES - N_GATES))).reshape(depth, 1, GATE_LANES)
    f1 = (vec(ffn1_norm_pre), vec(ffn1_norm_post), bf(ffn1_w_gate), bf(ffn1_w_up), bf(ffn1_w_down))
    f2 = (vec(ffn2_norm_pre), vec(ffn2_norm_post), bf(ffn2_w_gate), bf(ffn2_w_up), bf(ffn2_w_down))
    w_out_b = bf(w_out)
    cos2, sin2 = _rope_tables(s)
    x2 = x.reshape(b * s, d)
    for layer in range(depth):
        x2 = _ffn(x2, *f1, layer)
        x2 = _mixer(x2, b, s, layer, vec(mix_norm_pre), vec(mix_norm_post), w_main, w_gate, b_gate_pad,
                    conv_w, attn_sink, vec(mlstm_norm), w_out_b, cos2, sin2)
        x2 = _ffn(x2, *f2, layer)
    return x2.reshape(b, s, d)
```

```python
import functools

import jax
import jax.numpy as jnp
from jax import lax
from jax.experimental import pallas as pl
from jax.experimental.pallas import tpu as pltpu

F32 = jnp.float32
BF16 = jnp.bfloat16

EPS = 1e-6
ROPE_THETA = 10000.0
ATT_HEADS = 8
ATT_KV_HEADS = 2
ATT_GROUP = ATT_HEADS // ATT_KV_HEADS
ATT_HEAD_DIM = 128
ATT_BLOCK = 128
M_HEADS = 4
M_HEAD_DIM = 256
M_CHUNK = 256
M_NCOL = 128
CONV_WIDTH = 5
CONV_HALO = 16
CONV_ROW_BLOCK = 128
CONV_WINDOW = 256
GATE_LANES = 128
N_GATES = 4 * M_HEADS

V7X_VMEM_BYTES = 64 * 1024 * 1024
VMEM_LIMIT = 56 * 1024 * 1024


def _cparams(sem):
    return pltpu.CompilerParams(dimension_semantics=sem, vmem_limit_bytes=VMEM_LIMIT)


def _rms(x, g):
    return x * lax.rsqrt(jnp.mean(x * x, axis=-1, keepdims=True) + EPS) * g


def _ffn_body(xe_ref, xn_ref, gpre_ref, gpost_ref, wg_ref, wu_ref, wd_ref, o_ref, h_ref, acc_ref, *, n_chunks):
    r, j = pl.program_id(0), pl.program_id(1)
    last_r = pl.num_programs(0) - 1
    par = r % 2
    oth = 1 - par
    rc = xn_ref.shape[0]
    rows = pl.ds(pl.multiple_of(jnp.minimum(j, n_chunks - 1) * rc, rc), rc)
    g_pre = gpre_ref[...]
    g_post_half = 0.5 * gpost_ref[...]

    def prenorm_chunk():
        return _rms(xn_ref[...], g_pre).astype(BF16)

    def postnorm_chunk():
        return xe_ref[...] + _rms(acc_ref[par, rows, :], g_post_half)

    @pl.when(r == 0)
    def _():
        @pl.when(j == 0)
        def _():
            acc_ref[...] = jnp.zeros_like(acc_ref)
        h_ref[par, rows, :] = prenorm_chunk()

    @pl.when((r > 0) & (r < last_r))
    def _():
        out_chunk = postnorm_chunk()
        h_chunk = prenorm_chunk()
        h = h_ref[oth]
        g = jnp.dot(h, wg_ref[...], preferred_element_type=F32)
        u = jnp.dot(h, wu_ref[...], preferred_element_type=F32)
        a = (g * jax.nn.sigmoid(g) * u).astype(BF16)
        acc_ref[oth] = jnp.where(j == 0, 0.0, acc_ref[oth]) + jnp.dot(a, wd_ref[...], preferred_element_type=F32)
        o_ref[...] = out_chunk
        h_ref[par, rows, :] = h_chunk

    @pl.when(r == last_r)
    def _():
        o_ref[...] = postnorm_chunk()


def _ffn(x, gpre, gpost, wg, wu, wd, layer, tm=1024, tf=512, rows_per_step=128):
    t, d = x.shape
    f = wg.shape[-1]
    tm = min(tm, t)
    nt, nj = t // tm, f // tf
    n_chunks = tm // rows_per_step
    assert n_chunks <= nj, "every row chunk of a tile needs its own d_ff step"
    n_blocks = nt * n_chunks
    chunk = lambda j: jnp.minimum(j, n_chunks - 1)
    emit_map = lambda r, j: (jnp.clip((r - 2) * n_chunks + chunk(j), 0, n_blocks - 1), 0)
    next_map = lambda r, j: (jnp.minimum(r * n_chunks + chunk(j), n_blocks - 1), 0)
    wj = lambda r, j: jnp.where(r == 0, 0, jnp.where(r == nt + 1, nj - 1, j))

    return pl.pallas_call(
        functools.partial(_ffn_body, n_chunks=n_chunks),
        grid=(nt + 2, nj),
        in_specs=[
            pl.BlockSpec((rows_per_step, d), emit_map),
            pl.BlockSpec((rows_per_step, d), next_map),
            pl.BlockSpec((None, 1, d), lambda r, j: (layer, 0, 0)),
            pl.BlockSpec((None, 1, d), lambda r, j: (layer, 0, 0)),
            pl.BlockSpec((None, d, tf), lambda r, j: (layer, 0, wj(r, j))),
            pl.BlockSpec((None, d, tf), lambda r, j: (layer, 0, wj(r, j))),
            pl.BlockSpec((None, tf, d), lambda r, j: (layer, wj(r, j), 0)),
        ],
        out_specs=pl.BlockSpec((rows_per_step, d), emit_map),
        out_shape=jax.ShapeDtypeStruct((t, d), F32),
        scratch_shapes=[pltpu.VMEM((2, tm, d), BF16), pltpu.VMEM((2, tm, d), F32)],
        compiler_params=_cparams(("arbitrary", "arbitrary")),
        name="ffn",
    )(x, x, gpre, gpost, wg, wu, wd)


def _inproj_body(x_ref, g_ref, w_ref, wgate_ref, o_ref, og_ref, h_ref):
    r = pl.program_id(0)
    j = pl.program_id(1)
    par = r % 2
    rc = x_ref.shape[0]
    rows = pl.ds(pl.multiple_of(j * rc, rc), rc)
    h_chunk = _rms(x_ref[...], g_ref[...]).astype(BF16)
    og_ref[...] = jnp.dot(h_chunk, wgate_ref[...], preferred_element_type=F32)

    @pl.when(r > 0)
    def _():
        o_ref[...] = jnp.dot(h_ref[1 - par], w_ref[...], preferred_element_type=F32).astype(BF16)

    h_ref[par, rows, :] = h_chunk


def _inproj(x, g, w, wgate, layer, tm=512, tn=2816):
    t, d = x.shape
    n = w.shape[-1]
    tm = min(tm, t)
    nt, nj = t // tm, n // tn
    rc = tm // nj
    chunk_map = lambda r, j: (jnp.minimum(r * nj + j, nt * nj - 1), 0)
    return pl.pallas_call(
        _inproj_body,
        grid=(nt + 1, nj),
        in_specs=[
            pl.BlockSpec((rc, d), chunk_map),
            pl.BlockSpec((None, 1, d), lambda r, j: (layer, 0, 0)),
            pl.BlockSpec((None, d, tn), lambda r, j: (layer, 0, j)),
            pl.BlockSpec((None, d, GATE_LANES), lambda r, j: (layer, 0, 0)),
        ],
        out_specs=[
            pl.BlockSpec((tm, tn), lambda r, j: (jnp.maximum(r - 1, 0), jnp.where(r == 0, 0, j))),
            pl.BlockSpec((rc, GATE_LANES), chunk_map),
        ],
        out_shape=[
            jax.ShapeDtypeStruct((t, n), BF16),
            jax.ShapeDtypeStruct((t, GATE_LANES), F32),
        ],
        scratch_shapes=[pltpu.VMEM((2, tm, d), BF16)],
        compiler_params=_cparams(("arbitrary", "arbitrary")),
        name="inproj",
    )(x, g, w, wgate)


def _log_sigmoid(x):
    return -(jnp.maximum(-x, 0.0) + jnp.log1p(jnp.exp(-jnp.abs(x))))


def _gateprep_body(g_ref, bias_ref, gcol_ref, grow_ref, *, nch):
    L = M_CHUNK
    r = lax.broadcasted_iota(jnp.int32, (L, L), 0)
    c = lax.broadcasted_iota(jnp.int32, (L, L), 1)
    tril = (c <= r).astype(F32)
    triu = (c >= r).astype(F32)
    lane = lax.broadcasted_iota(jnp.int32, (L, GATE_LANES), 1)
    for ch in range(nch):
        x = g_ref[0, ch * L:(ch + 1) * L, :] + bias_ref[...]
        lf = _log_sigmoid(x)
        pre = jnp.dot(tril, lf, preferred_element_type=F32, precision=lax.Precision.HIGHEST)
        suf = jnp.dot(triu, lf, preferred_element_type=F32, precision=lax.Precision.HIGHEST)
        out = jnp.where(lane < 2 * M_HEADS, x, jnp.where(lane < 3 * M_HEADS, pre, suf))
        gcol_ref[0, ch * L:(ch + 1) * L, :] = out
        grow_ref[0, :, ch * L:(ch + 1) * L] = out.T[0:N_GATES, :]


def _gateprep(gates, bias, rows=1024):
    b, s, _ = gates.shape
    rows = min(rows, s)
    return pl.pallas_call(
        functools.partial(_gateprep_body, nch=rows // M_CHUNK),
        grid=(b, s // rows),
        in_specs=[
            pl.BlockSpec((1, rows, GATE_LANES), lambda i, j: (i, j, 0)),
            pl.BlockSpec((1, GATE_LANES), lambda i, j: (0, 0)),
        ],
        out_specs=[
            pl.BlockSpec((1, rows, GATE_LANES), lambda i, j: (i, j, 0)),
            pl.BlockSpec((1, N_GATES, rows), lambda i, j: (i, 0, j)),
        ],
        out_shape=[
            jax.ShapeDtypeStruct((b, s, GATE_LANES), F32),
            jax.ShapeDtypeStruct((b, N_GATES, s), F32),
        ],
        compiler_params=_cparams(("parallel", "parallel")),
        name="gateprep",
    )(gates, bias)


def _conv_body(xp_ref, xc_ref, xn_ref, w_ref, o_ref, xe_ref, *, ts):
    i = pl.program_id(1)
    c = pl.program_id(2)
    H, RB, WIN = CONV_HALO, CONV_ROW_BLOCK, CONV_WINDOW
    pad = CONV_WIDTH // 2
    tc = xc_ref.shape[-1]
    xe_ref[0:H, :] = jnp.where(i == 0, 0.0, xp_ref[0]).astype(BF16)
    xe_ref[H:H + ts, :] = xc_ref[0]
    xe_ref[H + ts:H + ts + H, :] = jnp.where(i == pl.num_programs(1) - 1, 0.0, xn_ref[0]).astype(BF16)
    xe_ref[H + ts + H:, :] = jnp.zeros((xe_ref.shape[0] - (ts + 2 * H), tc), BF16)

    taps = [k for k in range(CONV_WIDTH) if k != pad]
    diag = (lax.broadcasted_iota(jnp.int32, (RB, WIN), 1) - lax.broadcasted_iota(jnp.int32, (RB, WIN), 0))
    shift = jnp.concatenate([jnp.where(diag == H - pad + k, 1.0, 0.0).astype(BF16) for k in taps], axis=0)

    scale = jnp.where(c >= pl.num_programs(2) // 2, M_HEAD_DIM ** -0.5, 1.0)
    for b in range(ts // RB):
        window = xe_ref[b * RB:b * RB + WIN, :]
        shifted = jnp.dot(shift, window, preferred_element_type=F32)
        acc = xc_ref[0, b * RB:(b + 1) * RB, :].astype(F32) * w_ref[pad:pad + 1, :]
        for t, k in enumerate(taps):
            acc = acc + shifted[t * RB:(t + 1) * RB, :] * w_ref[k:k + 1, :]
        y = acc * jax.nn.sigmoid(acc)
        o_ref[0, b * RB:(b + 1) * RB, :] = (y * scale).astype(BF16)


def _conv_silu(proj, conv_w, layer, col0, ts=512, tc=512):
    b, s, _ = proj.shape
    ts = min(ts, s)
    width = conv_w.shape[-1]
    cb0 = col0 // tc
    hb = ts // CONV_HALO
    nhb = s // CONV_HALO
    assert CONV_WINDOW >= CONV_ROW_BLOCK + 2 * CONV_HALO and ts % CONV_ROW_BLOCK == 0
    return pl.pallas_call(
        functools.partial(_conv_body, ts=ts),
        grid=(b, s // ts, width // tc),
        in_specs=[
            pl.BlockSpec((1, CONV_HALO, tc), lambda bi, i, c: (bi, jnp.maximum(i * hb - 1, 0), cb0 + c)),
            pl.BlockSpec((1, ts, tc), lambda bi, i, c: (bi, i, cb0 + c)),
            pl.BlockSpec((1, CONV_HALO, tc), lambda bi, i, c: (bi, jnp.minimum((i + 1) * hb, nhb - 1), cb0 + c)),
            pl.BlockSpec((None, CONV_WIDTH, tc), lambda bi, i, c: (layer, 0, c)),
        ],
        out_specs=pl.BlockSpec((1, ts, tc), lambda bi, i, c: (bi, i, c)),
        out_shape=jax.ShapeDtypeStruct((b, s, width), BF16),
        scratch_shapes=[pltpu.VMEM((ts - CONV_ROW_BLOCK + CONV_WINDOW, tc), BF16)],
        compiler_params=_cparams(("parallel", "parallel", "parallel")),
        name="conv_silu",
    )(proj, proj, proj, conv_w)


def _rope(x_bf16, cos, sin_signed):
    n = x_bf16.shape[-1] // ATT_HEAD_DIM
    outs = []
    for h in range(n):
        x = x_bf16[:, h * ATT_HEAD_DIM:(h + 1) * ATT_HEAD_DIM].astype(F32)
        y = x * cos + pltpu.roll(x, ATT_HEAD_DIM // 2, axis=1) * sin_signed
        outs.append(y.astype(BF16))
    return outs


def _attn_body(sink_ref, q_ref, kp_ref, kc_ref, kn_ref, vp_ref, vc_ref, vn_ref,
               cp_ref, cc_ref, cn_ref, sp_ref, sc_ref, sn_ref, o_ref, *, tq, seq):
    i = pl.program_id(1)
    W = ATT_BLOCK
    nsb = tq // W
    qh = _rope(q_ref[0], cc_ref[...], sc_ref[...])
    kp = _rope(kp_ref[0], cp_ref[...], sp_ref[...])
    kc = _rope(kc_ref[0], cc_ref[...], sc_ref[...])
    kn = _rope(kn_ref[0], cn_ref[...], sn_ref[...])
    kwin = [jnp.concatenate([kp[h], kc[h], kn[h]], axis=0) for h in range(ATT_KV_HEADS)]
    vall = jnp.concatenate([vp_ref[0], vc_ref[0], vn_ref[0]], axis=0)
    r = lax.broadcasted_iota(jnp.int32, (W, 3 * W), 0)
    c = lax.broadcasted_iota(jnp.int32, (W, 3 * W), 1)
    band = jnp.abs(c - W - r) <= W
    scale = ATT_HEAD_DIM ** -0.5
    for sb in range(nsb):
        kpos = i * tq + (sb - 1) * W + c
        mask = band & (kpos >= 0) & (kpos < seq)
        for hk in range(ATT_KV_HEADS):
            q4 = jnp.concatenate(
                [qh[hk * ATT_GROUP + g][sb * W:(sb + 1) * W, :] for g in range(ATT_GROUP)], axis=0)
            kw = kwin[hk][sb * W:(sb + 3) * W, :]
            vw = vall[sb * W:(sb + 3) * W, hk * ATT_HEAD_DIM:(hk + 1) * ATT_HEAD_DIM]
            s = lax.dot_general(q4, kw, (((1,), (1,)), ((), ())), preferred_element_type=F32) * scale
            ps, dens = [], []
            for g in range(ATT_GROUP):
                sink = sink_ref[hk * ATT_GROUP + g]
                sg = jnp.where(mask, s[g * W:(g + 1) * W, :], -jnp.inf)
                m = jnp.maximum(jnp.max(sg, axis=-1, keepdims=True), sink)
                p = jnp.exp(sg - m)
                dens.append(jnp.sum(p, axis=-1, keepdims=True) + jnp.exp(sink - m))
                ps.append(p.astype(BF16))
            pv = jnp.dot(jnp.concatenate(ps, axis=0), vw, preferred_element_type=F32)
            for g in range(ATT_GROUP):
                h = hk * ATT_GROUP + g
                o_ref[0, sb * W:(sb + 1) * W, h * ATT_HEAD_DIM:(h + 1) * ATT_HEAD_DIM] = (
                    pv[g * W:(g + 1) * W, :] / dens[g]).astype(BF16)


def _attention(proj, cos2, sin2, sink, tq=512):
    b, s, _ = proj.shape
    tq = min(tq, s)
    W = ATT_BLOCK
    aw = ATT_HEADS * ATT_HEAD_DIM
    kvw = ATT_KV_HEADS * ATT_HEAD_DIM
    kb, vb = aw // kvw, aw // kvw + 1
    nb = tq // W
    nwb = s // W
    prev = lambda i: jnp.maximum(i * nb - 1, 0)
    nxt = lambda i: jnp.minimum((i + 1) * nb, nwb - 1)
    return pl.pallas_call(
        functools.partial(_attn_body, tq=tq, seq=s),
        grid=(b, s // tq),
        in_specs=[
            pl.BlockSpec(memory_space=pltpu.SMEM),
            pl.BlockSpec((1, tq, aw), lambda bi, i: (bi, i, 0)),
            pl.BlockSpec((1, W, kvw), lambda bi, i: (bi, prev(i), kb)),
            pl.BlockSpec((1, tq, kvw), lambda bi, i: (bi, i, kb)),
            pl.BlockSpec((1, W, kvw), lambda bi, i: (bi, nxt(i), kb)),
            pl.BlockSpec((1, W, kvw), lambda bi, i: (bi, prev(i), vb)),
            pl.BlockSpec((1, tq, kvw), lambda bi, i: (bi, i, vb)),
            pl.BlockSpec((1, W, kvw), lambda bi, i: (bi, nxt(i), vb)),
            pl.BlockSpec((W, ATT_HEAD_DIM), lambda bi, i: (prev(i), 0)),
            pl.BlockSpec((tq, ATT_HEAD_DIM), lambda bi, i: (i, 0)),
            pl.BlockSpec((W, ATT_HEAD_DIM), lambda bi, i: (nxt(i), 0)),
            pl.BlockSpec((W, ATT_HEAD_DIM), lambda bi, i: (prev(i), 0)),
            pl.BlockSpec((tq, ATT_HEAD_DIM), lambda bi, i: (i, 0)),
            pl.BlockSpec((W, ATT_HEAD_DIM), lambda bi, i: (nxt(i), 0)),
        ],
        out_specs=pl.BlockSpec((1, tq, aw), lambda bi, i: (bi, i, 0)),
        out_shape=jax.ShapeDtypeStruct((b, s, aw), BF16),
        compiler_params=_cparams(("parallel", "parallel")),
        name="attention",
    )(sink, proj, proj, proj, proj, proj, proj, proj, cos2, cos2, cos2, sin2, sin2, sin2)


def _rep2(x):
    return jnp.concatenate([x, x], axis=1)


def _mlstm_body(qf_ref, kf_ref, vf0_ref, vf1_ref, gcf_ref, grf_ref,
                qb_ref, kb_ref, vb0_ref, vb1_ref, gcb_ref, grb_ref,
                hf_ref, hb_ref, st_ref, m_ref, *, chains_per_group):
    @pl.when(pl.program_id(1) == 0)
    def _():
        st_ref[...] = jnp.zeros_like(st_ref)
        m_ref[...] = jnp.zeros_like(m_ref)

    L = qf_ref.shape[1]
    dirs = ((qf_ref, kf_ref, (vf0_ref, vf1_ref), gcf_ref, grf_ref, hf_ref),
            (qb_ref, kb_ref, (vb0_ref, vb1_ref), gcb_ref, grb_ref, hb_ref))
    r = lax.broadcasted_iota(jnp.int32, (L, L), 0)
    c = lax.broadcasted_iota(jnp.int32, (L, L), 1)
    tris = (c <= r, c >= r)
    all_chains = [(rev, h) for rev in range(2) for h in range(M_HEADS)]
    for g0 in range(0, len(all_chains), chains_per_group):
        _mlstm_group(all_chains[g0:g0 + chains_per_group], dirs, tris, st_ref, m_ref)


def _mlstm_group(chains, dirs, tris, st_ref, m_ref):
    D = M_HEAD_DIM
    R = M_NCOL
    L = tris[0].shape[0]
    slot = [rev * M_HEADS + h for rev, h in chains]
    q, k, v, bc, igc, arow, m_prev = [], [], [], [], [], [], []
    for rev, h in chains:
        q_ref, k_ref, v_refs, gc_ref, gr_ref, _ = dirs[rev]
        ig_lane = rev * M_HEADS + h
        b_lane = (2 + rev) * M_HEADS + h
        q.append(q_ref[0, :, h * D:(h + 1) * D])
        k.append(k_ref[0, :, h * D:(h + 1) * D])
        v.append(v_refs[h // 2][0, :, (h % 2) * D:(h % 2 + 1) * D])
        bc.append(jnp.broadcast_to(gc_ref[0, :, b_lane:b_lane + 1], (L, R)))
        igc.append(jnp.broadcast_to(gc_ref[0, :, ig_lane:ig_lane + 1], (L, R)))
        arow.append(gr_ref[0, ig_lane:ig_lane + 1, :] - gr_ref[0, b_lane:b_lane + 1, :])
        m_prev.append(m_ref[rev * M_HEADS + h][0:1, :])

    n = len(chains)
    log_d = [jnp.where(tris[chains[i][0]], _rep2(bc[i]) + arow[i], -jnp.inf) for i in range(n)]
    row_max = [jnp.broadcast_to(jnp.max(log_d[i], axis=-1, keepdims=True), (L, R)) for i in range(n)]
    log_inter = [bc[i] + m_prev[i] for i in range(n)]
    m_t = [jnp.maximum(log_inter[i], row_max[i]) for i in range(n)]
    d_mat = [jnp.exp(log_d[i] - _rep2(m_t[i])) for i in range(n)]
    inter = [jnp.exp(log_inter[i] - m_t[i]) for i in range(n)]
    qk = [lax.dot_general(q[i], k[i], (((1,), (1,)), ((), ())), preferred_element_type=F32) for i in range(n)]
    s = [qk[i] * d_mat[i] for i in range(n)]
    p2 = [jnp.dot(q[i], st_ref[slot[i]].astype(BF16), preferred_element_type=F32) for i in range(n)]
    sv = [jnp.dot(s[i].astype(BF16), v[i], preferred_element_type=F32) for i in range(n)]
    for i, (rev, h) in enumerate(chains):
        num = sv[i] + _rep2(inter[i]) * p2[i][:, :D]
        den = jnp.broadcast_to(jnp.sum(s[i], axis=-1, keepdims=True), (L, R)) + inter[i] * p2[i][:, D:]
        scale = 1.0 / jnp.maximum(jnp.abs(den), jnp.exp(-m_t[i]))
        dirs[rev][5][0, :, h * D:(h + 1) * D] = num * _rep2(scale)

    for i, (rev, h) in enumerate(chains):
        b_last = bc[i][0:1, :] if rev else bc[i][L - 1:L, :]
        log_w = b_last - bc[i] + igc[i]
        m_new = jnp.maximum(b_last + m_prev[i], jnp.max(log_w, axis=0, keepdims=True))
        w = jnp.exp(log_w - m_new)
        decay = jnp.exp(b_last + m_prev[i] - m_new)
        wext = jnp.concatenate([(_rep2(w) * v[i].astype(F32)).astype(BF16), w.astype(BF16)], axis=1)
        upd = lax.dot_general(k[i], wext, (((0,), (0,)), ((), ())), preferred_element_type=F32)
        st_ref[slot[i]] = jnp.concatenate([decay, decay, decay], axis=1) * st_ref[slot[i]] + upd
        m_ref[slot[i]] = jnp.broadcast_to(m_new, m_ref.shape[1:])


def _mlstm(qk, proj, gcol, grow, vcol0, chains_per_group=2):
    b, s, _ = qk.shape
    L = M_CHUNK
    nc = s // L
    mw = M_HEADS * M_HEAD_DIM
    vw = mw // 2
    vb = vcol0 // vw
    fwd = lambda i: i
    bwd = lambda i: nc - 1 - i

    def specs(pos):
        return [
            pl.BlockSpec((1, L, mw), lambda bi, i: (bi, pos(i), 0)),
            pl.BlockSpec((1, L, mw), lambda bi, i: (bi, pos(i), 1)),
            pl.BlockSpec((1, L, vw), lambda bi, i: (bi, pos(i), vb)),
            pl.BlockSpec((1, L, vw), lambda bi, i: (bi, pos(i), vb + 1)),
            pl.BlockSpec((1, L, GATE_LANES), lambda bi, i: (bi, pos(i), 0)),
            pl.BlockSpec((1, N_GATES, L), lambda bi, i: (bi, 0, pos(i))),
        ]

    nchain = 2 * M_HEADS
    return pl.pallas_call(
        functools.partial(_mlstm_body, chains_per_group=chains_per_group),
        grid=(b, nc),
        in_specs=specs(fwd) + specs(bwd),
        out_specs=[
            pl.BlockSpec((1, L, mw), lambda bi, i: (bi, fwd(i), 0)),
            pl.BlockSpec((1, L, mw), lambda bi, i: (bi, bwd(i), 0)),
        ],
        out_shape=[jax.ShapeDtypeStruct((b, s, mw), F32), jax.ShapeDtypeStruct((b, s, mw), F32)],
        scratch_shapes=[
            pltpu.VMEM((nchain, M_HEAD_DIM, M_HEAD_DIM + M_NCOL), F32),
            pltpu.VMEM((nchain, 8, 128), F32),
        ],
        compiler_params=_cparams(("parallel", "arbitrary")),
        name="mlstm",
    )(qk, qk, proj, proj, gcol, grow, qk, qk, proj, proj, gcol, grow)


def _outproj_body(x_ref, ya_ref, hf_ref, hb_ref, o0_ref, o1_ref, mn_ref, w_ref, g_ref, o_ref):
    D = M_HEAD_DIM
    hm = hf_ref[...] + hb_ref[...]
    ys = []
    for h in range(M_HEADS):
        o_gate = (o0_ref, o1_ref)[h // 2][:, (h % 2) * D:(h % 2 + 1) * D].astype(F32)
        y = _rms(hm[:, h * D:(h + 1) * D], mn_ref[:, h * D:(h + 1) * D])
        ys.append((jax.nn.sigmoid(o_gate) * y).astype(BF16))
    ym = jnp.concatenate(ys, axis=-1)
    aw = ya_ref.shape[-1]
    m = jnp.dot(ya_ref[...], w_ref[0:aw, :], preferred_element_type=F32)
    m = m + jnp.dot(ym, w_ref[aw:, :], preferred_element_type=F32)
    o_ref[...] = x_ref[...] + _rms(m, g_ref[...])


def _outproj(x, ya, hf, hb, proj, mnorm, w, g, layer, ocol0, tm=512):
    t, d = x.shape
    aw = ya.shape[-1]
    mw = hf.shape[-1]
    ow = mw // 2
    ob = ocol0 // ow
    return pl.pallas_call(
        _outproj_body,
        grid=(t // tm,),
        in_specs=[
            pl.BlockSpec((tm, d), lambda i: (i, 0)),
            pl.BlockSpec((tm, aw), lambda i: (i, 0)),
            pl.BlockSpec((tm, mw), lambda i: (i, 0)),
            pl.BlockSpec((tm, mw), lambda i: (i, 0)),
            pl.BlockSpec((tm, ow), lambda i: (i, ob)),
            pl.BlockSpec((tm, ow), lambda i: (i, ob + 1)),
            pl.BlockSpec((None, 1, mw), lambda i: (layer, 0, 0)),
            pl.BlockSpec((None, aw + mw, d), lambda i: (layer, 0, 0)),
            pl.BlockSpec((None, 1, d), lambda i: (layer, 0, 0)),
        ],
        out_specs=pl.BlockSpec((tm, d), lambda i: (i, 0)),
        out_shape=jax.ShapeDtypeStruct((t, d), F32),
        compiler_params=_cparams(("parallel",)),
        name="outproj",
    )(x, ya, hf, hb, proj, proj, mnorm, w, g)


def _rope_tables(seq):
    half = ATT_HEAD_DIM // 2
    pos = jnp.arange(seq, dtype=F32)
    inv_freq = ROPE_THETA ** (-jnp.arange(half, dtype=F32) / half)
    ang = pos[:, None] * inv_freq[None, :]
    cos, sin = jnp.cos(ang), jnp.sin(ang)
    return jnp.concatenate([cos, cos], axis=-1), jnp.concatenate([-sin, sin], axis=-1)


def _mixer(x2, b, s, layer, mix_norm_pre, mix_norm_post, w_main, w_gate, b_gate_pad, conv_w,
           attn_sink, mlstm_norm, w_out, cos2, sin2):
    t, d = x2.shape
    att_w = ATT_HEADS * ATT_HEAD_DIM
    kv_w = ATT_KV_HEADS * ATT_HEAD_DIM
    m_w = M_HEADS * M_HEAD_DIM
    qm0 = att_w + 2 * kv_w
    vm0 = qm0 + 2 * m_w
    om0 = vm0 + m_w
    proj, gates = _inproj(x2, mix_norm_pre, w_main, w_gate, layer)
    proj3 = proj.reshape(b, s, proj.shape[-1])
    gcol, grow = _gateprep(gates.reshape(b, s, GATE_LANES), b_gate_pad[layer])
    qk = _conv_silu(proj3, conv_w, layer, qm0)
    ya = _attention(proj3, cos2, sin2, attn_sink[layer])
    hf, hb = _mlstm(qk, proj3, gcol, grow, vm0)
    return _outproj(x2, ya.reshape(t, att_w), hf.reshape(t, m_w), hb.reshape(t, m_w), proj,
                    mlstm_norm, w_out, mix_norm_post, layer, om0)


def kernel(x, ffn1_norm_pre, ffn1_norm_post, ffn1_w_gate, ffn1_w_up, ffn1_w_down, mix_norm_pre, mix_norm_post, w_in, b_gate, conv_w, attn_sink, mlstm_norm, w_out, ffn2_norm_pre, ffn2_norm_post, ffn2_w_gate, ffn2_w_up, ffn2_w_down):
    b, s, d = x.shape
    depth = w_in.shape[0]
    n_main = w_in.shape[-1] - N_GATES
    vec = lambda g: g.reshape(depth, 1, g.shape[-1])
    bf = lambda w: w.astype(BF16)
    w_main = bf(w_in[:, :, :n_main])
    w_gate = bf(jnp.pad(w_in[:, :, n_main:], ((0, 0), (0, 0), (0, GATE_LANES - N_GATES))))
    b_gate_pad = jnp.pad(b_gate, ((0, 0), (0, GATE_LANES - N_GATES))).reshape(depth, 1, GATE_LANES)
    f1 = (vec(ffn1_norm_pre), vec(ffn1_norm_post), bf(ffn1_w_gate), bf(ffn1_w_up), bf(ffn1_w_down))
    f2 = (vec(ffn2_norm_pre), vec(ffn2_norm_post), bf(ffn2_w_gate), bf(ffn2_w_up), bf(ffn2_w_down))
    w_out_b = bf(w_out)
    cos2, sin2 = _rope_tables(s)
    x2 = x.reshape(b * s, d)
    for layer in range(depth):
        x2 = _ffn(x2, *f1, layer)
        x2 = _mixer(x2, b, s, layer, vec(mix_norm_pre), vec(mix_norm_post), w_main, w_gate, b_gate_pad,
                    conv_w, attn_sink, vec(mlstm_norm), w_out_b, cos2, sin2)
        x2 = _ffn(x2, *f2, layer)
    return x2.reshape(b, s, d)
```

```python
import functools

import jax
import jax.numpy as jnp
from jax import lax
from jax.experimental import pallas as pl
from jax.experimental.pallas import tpu as pltpu

F32 = jnp.float32
BF16 = jnp.bfloat16

EPS = 1e-6
ROPE_THETA = 10000.0
ATT_HEADS = 8
ATT_KV_HEADS = 2
ATT_GROUP = ATT_HEADS // ATT_KV_HEADS
ATT_HEAD_DIM = 128
ATT_BLOCK = 128
M_HEADS = 4
M_HEAD_DIM = 256
M_CHUNK = 256
M_NCOL = 128
CONV_WIDTH = 5
CONV_HALO = 16
CONV_ROW_BLOCK = 128
CONV_WINDOW = 256
GATE_LANES = 128
N_GATES = 4 * M_HEADS

V7X_VMEM_BYTES = 64 * 1024 * 1024
VMEM_LIMIT = 56 * 1024 * 1024


def _cparams(sem):
    return pltpu.CompilerParams(dimension_semantics=sem, vmem_limit_bytes=VMEM_LIMIT)


def _rms(x, g):
    return x * lax.rsqrt(jnp.mean(x * x, axis=-1, keepdims=True) + EPS) * g


def _ffn_body(xe_ref, xn_ref, gpre_ref, gpost_ref, wg_ref, wu_ref, wd_ref, *rest, n_chunks, n_side):
    side_in, (o_ref, *side_out), (h_ref, acc_ref) = rest[:n_side], rest[n_side:2 * n_side + 1], rest[2 * n_side + 1:]
    r, j = pl.program_id(0), pl.program_id(1)
    last_r = pl.num_programs(0) - 1
    par = r % 2
    oth = 1 - par
    rc = xn_ref.shape[0]
    rows = pl.ds(pl.multiple_of(jnp.minimum(j, n_chunks - 1) * rc, rc), rc)
    g_pre = gpre_ref[...]
    g_post_half = 0.5 * gpost_ref[...]

    def prenorm_chunk():
        return _rms(xn_ref[...], g_pre).astype(BF16)

    def postnorm_chunk():
        return xe_ref[...] + _rms(acc_ref[par, rows, :], g_post_half)

    @pl.when(r == 0)
    def _():
        @pl.when(j == 0)
        def _():
            acc_ref[...] = jnp.zeros_like(acc_ref)
        h_ref[par, rows, :] = prenorm_chunk()

    @pl.when((r > 0) & (r < last_r))
    def _():
        out_chunk = postnorm_chunk()
        h_chunk = prenorm_chunk()
        h = h_ref[oth]
        g = jnp.dot(h, wg_ref[...], preferred_element_type=F32)
        u = jnp.dot(h, wu_ref[...], preferred_element_type=F32)
        a = (g * jax.nn.sigmoid(g) * u).astype(BF16)
        acc_ref[oth] = jnp.where(j == 0, 0.0, acc_ref[oth]) + jnp.dot(a, wd_ref[...], preferred_element_type=F32)
        o_ref[...] = out_chunk
        h_ref[par, rows, :] = h_chunk
        for src_ref, dst_ref in zip(side_in, side_out):
            dst_ref[...] = src_ref[...].astype(BF16)

    @pl.when(r == last_r)
    def _():
        o_ref[...] = postnorm_chunk()


def _side_rows(n_rows, max_blocks):
    rb = 16
    while n_rows % rb or n_rows // rb > max_blocks:
        rb += 16
    return rb


def _ffn(x, gpre, gpost, wg, wu, wd, layer, side=None, tm=1024, tf=512, rows_per_step=128):
    t, d = x.shape
    f = wg.shape[-1]
    tm = min(tm, t)
    nt, nj = t // tm, f // tf
    n_chunks = tm // rows_per_step
    assert n_chunks <= nj, "every row chunk of a tile needs its own d_ff step"
    n_blocks = nt * n_chunks
    chunk = lambda j: jnp.minimum(j, n_chunks - 1)
    emit_map = lambda r, j: (jnp.clip((r - 2) * n_chunks + chunk(j), 0, n_blocks - 1), 0)
    next_map = lambda r, j: (jnp.minimum(r * n_chunks + chunk(j), n_blocks - 1), 0)
    wj = lambda r, j: jnp.where(r == 0, 0, jnp.where(r == nt + 1, nj - 1, j))

    side_w, side_layer = side if side is not None else ((), 0)
    side_in_specs, side_out_specs, side_shapes = [], [], []
    for w in side_w:
        _, n_rows, n_cols = w.shape
        rb = _side_rows(n_rows, nt * nj)
        blk = lambda r, j, nb=n_rows // rb: jnp.clip((r - 1) * nj + j, 0, nb - 1)
        side_in_specs.append(pl.BlockSpec((None, rb, n_cols), lambda r, j, blk=blk: (side_layer, blk(r, j), 0)))
        side_out_specs.append(pl.BlockSpec((rb, n_cols), lambda r, j, blk=blk: (blk(r, j), 0)))
        side_shapes.append(jax.ShapeDtypeStruct((n_rows, n_cols), BF16))

    out, *side_out = pl.pallas_call(
        functools.partial(_ffn_body, n_chunks=n_chunks, n_side=len(side_w)),
        grid=(nt + 2, nj),
        in_specs=[
            pl.BlockSpec((rows_per_step, d), emit_map),
            pl.BlockSpec((rows_per_step, d), next_map),
            pl.BlockSpec((None, 1, d), lambda r, j: (layer, 0, 0)),
            pl.BlockSpec((None, 1, d), lambda r, j: (layer, 0, 0)),
            pl.BlockSpec((d, tf), lambda r, j: (0, wj(r, j))),
            pl.BlockSpec((d, tf), lambda r, j: (0, wj(r, j))),
            pl.BlockSpec((tf, d), lambda r, j: (wj(r, j), 0)),
        ] + side_in_specs,
        out_specs=[pl.BlockSpec((rows_per_step, d), emit_map)] + side_out_specs,
        out_shape=[jax.ShapeDtypeStruct((t, d), F32)] + side_shapes,
        scratch_shapes=[pltpu.VMEM((2, tm, d), BF16), pltpu.VMEM((2, tm, d), F32)],
        compiler_params=_cparams(("arbitrary", "arbitrary")),
        name="ffn",
    )(x, x, gpre, gpost, wg, wu, wd, *side_w)
    return out, tuple(side_out)


def _inproj_body(x_ref, g_ref, w_ref, wgate_ref, o_ref, og_ref, h_ref):
    r = pl.program_id(0)
    j = pl.program_id(1)
    par = r % 2
    rc = x_ref.shape[0]
    rows = pl.ds(pl.multiple_of(j * rc, rc), rc)
    h_chunk = _rms(x_ref[...], g_ref[...]).astype(BF16)
    og_ref[...] = jnp.dot(h_chunk, wgate_ref[...], preferred_element_type=F32)

    @pl.when(r > 0)
    def _():
        o_ref[...] = jnp.dot(h_ref[1 - par], w_ref[...], preferred_element_type=F32).astype(BF16)

    h_ref[par, rows, :] = h_chunk


def _inproj(x, g, w, wgate, layer, tm=512, tn=2816):
    t, d = x.shape
    n = w.shape[-1]
    tm = min(tm, t)
    nt, nj = t // tm, n // tn
    rc = tm // nj
    chunk_map = lambda r, j: (jnp.minimum(r * nj + j, nt * nj - 1), 0)
    return pl.pallas_call(
        _inproj_body,
        grid=(nt + 1, nj),
        in_specs=[
            pl.BlockSpec((rc, d), chunk_map),
            pl.BlockSpec((None, 1, d), lambda r, j: (layer, 0, 0)),
            pl.BlockSpec((None, d, tn), lambda r, j: (layer, 0, j)),
            pl.BlockSpec((None, d, GATE_LANES), lambda r, j: (layer, 0, 0)),
        ],
        out_specs=[
            pl.BlockSpec((tm, tn), lambda r, j: (jnp.maximum(r - 1, 0), jnp.where(r == 0, 0, j))),
            pl.BlockSpec((rc, GATE_LANES), chunk_map),
        ],
        out_shape=[
            jax.ShapeDtypeStruct((t, n), BF16),
            jax.ShapeDtypeStruct((t, GATE_LANES), F32),
        ],
        scratch_shapes=[pltpu.VMEM((2, tm, d), BF16)],
        compiler_params=_cparams(("arbitrary", "arbitrary")),
        name="inproj",
    )(x, g, w, wgate)


def _log_sigmoid(x):
    return -(jnp.maximum(-x, 0.0) + jnp.log1p(jnp.exp(-jnp.abs(x))))


def _gateprep_body(g_ref, bias_ref, gcol_ref, grow_ref, *, nch):
    L = M_CHUNK
    r = lax.broadcasted_iota(jnp.int32, (L, L), 0)
    c = lax.broadcasted_iota(jnp.int32, (L, L), 1)
    tril = (c <= r).astype(F32)
    triu = (c >= r).astype(F32)
    lane = lax.broadcasted_iota(jnp.int32, (L, GATE_LANES), 1)
    for ch in range(nch):
        x = g_ref[0, ch * L:(ch + 1) * L, :] + bias_ref[...]
        lf = _log_sigmoid(x)
        pre = jnp.dot(tril, lf, preferred_element_type=F32, precision=lax.Precision.HIGHEST)
        suf = jnp.dot(triu, lf, preferred_element_type=F32, precision=lax.Precision.HIGHEST)
        out = jnp.where(lane < 2 * M_HEADS, x, jnp.where(lane < 3 * M_HEADS, pre, suf))
        gcol_ref[0, ch * L:(ch + 1) * L, :] = out
        grow_ref[0, :, ch * L:(ch + 1) * L] = out.T[0:N_GATES, :]


def _gateprep(gates, bias, rows=1024):
    b, s, _ = gates.shape
    rows = min(rows, s)
    return pl.pallas_call(
        functools.partial(_gateprep_body, nch=rows // M_CHUNK),
        grid=(b, s // rows),
        in_specs=[
            pl.BlockSpec((1, rows, GATE_LANES), lambda i, j: (i, j, 0)),
            pl.BlockSpec((1, GATE_LANES), lambda i, j: (0, 0)),
        ],
        out_specs=[
            pl.BlockSpec((1, rows, GATE_LANES), lambda i, j: (i, j, 0)),
            pl.BlockSpec((1, N_GATES, rows), lambda i, j: (i, 0, j)),
        ],
        out_shape=[
            jax.ShapeDtypeStruct((b, s, GATE_LANES), F32),
            jax.ShapeDtypeStruct((b, N_GATES, s), F32),
        ],
        compiler_params=_cparams(("parallel", "parallel")),
        name="gateprep",
    )(gates, bias)


def _conv_body(xp_ref, xc_ref, xn_ref, w_ref, o_ref, xe_ref, *, ts):
    i = pl.program_id(1)
    c = pl.program_id(2)
    H, RB, WIN = CONV_HALO, CONV_ROW_BLOCK, CONV_WINDOW
    pad = CONV_WIDTH // 2
    tc = xc_ref.shape[-1]
    xe_ref[0:H, :] = jnp.where(i == 0, 0.0, xp_ref[0]).astype(BF16)
    xe_ref[H:H + ts, :] = xc_ref[0]
    xe_ref[H + ts:H + ts + H, :] = jnp.where(i == pl.num_programs(1) - 1, 0.0, xn_ref[0]).astype(BF16)
    xe_ref[H + ts + H:, :] = jnp.zeros((xe_ref.shape[0] - (ts + 2 * H), tc), BF16)

    taps = [k for k in range(CONV_WIDTH) if k != pad]
    diag = (lax.broadcasted_iota(jnp.int32, (RB, WIN), 1) - lax.broadcasted_iota(jnp.int32, (RB, WIN), 0))
    shift = jnp.concatenate([jnp.where(diag == H - pad + k, 1.0, 0.0).astype(BF16) for k in taps], axis=0)

    scale = jnp.where(c >= pl.num_programs(2) // 2, M_HEAD_DIM ** -0.5, 1.0)
    for b in range(ts // RB):
        window = xe_ref[b * RB:b * RB + WIN, :]
        shifted = jnp.dot(shift, window, preferred_element_type=F32)
        acc = xc_ref[0, b * RB:(b + 1) * RB, :].astype(F32) * w_ref[pad:pad + 1, :]
        for t, k in enumerate(taps):
            acc = acc + shifted[t * RB:(t + 1) * RB, :] * w_ref[k:k + 1, :]
        y = acc * jax.nn.sigmoid(acc)
        o_ref[0, b * RB:(b + 1) * RB, :] = (y * scale).astype(BF16)


def _conv_silu(proj, conv_w, layer, col0, ts=1024, tc=512):
    b, s, _ = proj.shape
    ts = min(ts, s)
    width = conv_w.shape[-1]
    cb0 = col0 // tc
    hb = ts // CONV_HALO
    nhb = s // CONV_HALO
    assert CONV_WINDOW >= CONV_ROW_BLOCK + 2 * CONV_HALO and ts % CONV_ROW_BLOCK == 0
    return pl.pallas_call(
        functools.partial(_conv_body, ts=ts),
        grid=(b, s // ts, width // tc),
        in_specs=[
            pl.BlockSpec((1, CONV_HALO, tc), lambda bi, i, c: (bi, jnp.maximum(i * hb - 1, 0), cb0 + c)),
            pl.BlockSpec((1, ts, tc), lambda bi, i, c: (bi, i, cb0 + c)),
            pl.BlockSpec((1, CONV_HALO, tc), lambda bi, i, c: (bi, jnp.minimum((i + 1) * hb, nhb - 1), cb0 + c)),
            pl.BlockSpec((None, CONV_WIDTH, tc), lambda bi, i, c: (layer, 0, c)),
        ],
        out_specs=pl.BlockSpec((1, ts, tc), lambda bi, i, c: (bi, i, c)),
        out_shape=jax.ShapeDtypeStruct((b, s, width), BF16),
        scratch_shapes=[pltpu.VMEM((ts - CONV_ROW_BLOCK + CONV_WINDOW, tc), BF16)],
        compiler_params=_cparams(("parallel", "parallel", "parallel")),
        name="conv_silu",
    )(proj, proj, proj, conv_w)


def _rope(x_bf16, cos, sin_signed):
    n = x_bf16.shape[-1] // ATT_HEAD_DIM
    outs = []
    for h in range(n):
        x = x_bf16[:, h * ATT_HEAD_DIM:(h + 1) * ATT_HEAD_DIM].astype(F32)
        y = x * cos + pltpu.roll(x, ATT_HEAD_DIM // 2, axis=1) * sin_signed
        outs.append(y.astype(BF16))
    return outs


def _attn_body(sink_ref, q_ref, kp_ref, kc_ref, kn_ref, vp_ref, vc_ref, vn_ref,
               cp_ref, cc_ref, cn_ref, sp_ref, sc_ref, sn_ref, o_ref, *, tq):
    i = pl.program_id(1)
    W = ATT_BLOCK
    nsb = tq // W
    qh = _rope(q_ref[0], cc_ref[...], sc_ref[...])
    kp = _rope(kp_ref[0], cp_ref[...], sp_ref[...])
    kc = _rope(kc_ref[0], cc_ref[...], sc_ref[...])
    kn = _rope(kn_ref[0], cn_ref[...], sn_ref[...])
    kwin = [jnp.concatenate([kp[h], kc[h], kn[h]], axis=0) for h in range(ATT_KV_HEADS)]
    vall = jnp.concatenate([vp_ref[0], vc_ref[0], vn_ref[0]], axis=0)
    r = lax.broadcasted_iota(jnp.int32, (W, W), 0)
    c = lax.broadcasted_iota(jnp.int32, (W, W), 1)
    first_block = i == 0
    last_block = i == pl.num_programs(1) - 1
    scale = ATT_HEAD_DIM ** -0.5
    units = [(sb, hk) for sb in range(nsb) for hk in range(ATT_KV_HEADS)]

    def masks(sb):
        left = c >= r
        right = c <= r
        if sb == 0:
            left = left & jnp.logical_not(first_block)
        if sb == nsb - 1:
            right = right & jnp.logical_not(last_block)
        return left, right

    scores = []
    for sb, hk in units:
        q4 = jnp.concatenate(
            [qh[hk * ATT_GROUP + g][sb * W:(sb + 1) * W, :] for g in range(ATT_GROUP)], axis=0)
        kw = kwin[hk][sb * W:(sb + 3) * W, :]
        scores.append(lax.dot_general(q4, kw, (((1,), (1,)), ((), ())), preferred_element_type=F32) * scale)

    probs, dens = [], []
    for (sb, hk), s in zip(units, scores):
        left, right = masks(sb)
        ps, ds = [], []
        for g in range(ATT_GROUP):
            sink = sink_ref[hk * ATT_GROUP + g]
            sg = s[g * W:(g + 1) * W, :]
            sl = jnp.where(left, sg[:, :W], -jnp.inf)
            sm = sg[:, W:2 * W]
            sr = jnp.where(right, sg[:, 2 * W:], -jnp.inf)
            m = jnp.maximum(jnp.max(jnp.maximum(jnp.maximum(sl, sm), sr), axis=-1, keepdims=True), sink)
            pl_, pm, pr = jnp.exp(sl - m), jnp.exp(sm - m), jnp.exp(sr - m)
            ds.append(jnp.sum(pl_ + pm + pr, axis=-1, keepdims=True) + jnp.exp(sink - m))
            ps.append(jnp.concatenate([pl_, pm, pr], axis=1).astype(BF16))
        probs.append(jnp.concatenate(ps, axis=0))
        dens.append(ds)

    outs = []
    for (sb, hk), p in zip(units, probs):
        vw = vall[sb * W:(sb + 3) * W, hk * ATT_HEAD_DIM:(hk + 1) * ATT_HEAD_DIM]
        outs.append(jnp.dot(p, vw, preferred_element_type=F32))

    for (sb, hk), pv, ds in zip(units, outs, dens):
        for g in range(ATT_GROUP):
            h = hk * ATT_GROUP + g
            o_ref[0, sb * W:(sb + 1) * W, h * ATT_HEAD_DIM:(h + 1) * ATT_HEAD_DIM] = (
                pv[g * W:(g + 1) * W, :] / ds[g]).astype(BF16)


def _attention(proj, cos2, sin2, sink, tq=512):
    b, s, _ = proj.shape
    tq = min(tq, s)
    W = ATT_BLOCK
    aw = ATT_HEADS * ATT_HEAD_DIM
    kvw = ATT_KV_HEADS * ATT_HEAD_DIM
    kb, vb = aw // kvw, aw // kvw + 1
    nb = tq // W
    nwb = s // W
    prev = lambda i: jnp.maximum(i * nb - 1, 0)
    nxt = lambda i: jnp.minimum((i + 1) * nb, nwb - 1)
    return pl.pallas_call(
        functools.partial(_attn_body, tq=tq),
        grid=(b, s // tq),
        in_specs=[
            pl.BlockSpec(memory_space=pltpu.SMEM),
            pl.BlockSpec((1, tq, aw), lambda bi, i: (bi, i, 0)),
            pl.BlockSpec((1, W, kvw), lambda bi, i: (bi, prev(i), kb)),
            pl.BlockSpec((1, tq, kvw), lambda bi, i: (bi, i, kb)),
            pl.BlockSpec((1, W, kvw), lambda bi, i: (bi, nxt(i), kb)),
            pl.BlockSpec((1, W, kvw), lambda bi, i: (bi, prev(i), vb)),
            pl.BlockSpec((1, tq, kvw), lambda bi, i: (bi, i, vb)),
            pl.BlockSpec((1, W, kvw), lambda bi, i: (bi, nxt(i), vb)),
            pl.BlockSpec((W, ATT_HEAD_DIM), lambda bi, i: (prev(i), 0)),
            pl.BlockSpec((tq, ATT_HEAD_DIM), lambda bi, i: (i, 0)),
            pl.BlockSpec((W, ATT_HEAD_DIM), lambda bi, i: (nxt(i), 0)),
            pl.BlockSpec((W, ATT_HEAD_DIM), lambda bi, i: (prev(i), 0)),
            pl.BlockSpec((tq, ATT_HEAD_DIM), lambda bi, i: (i, 0)),
            pl.BlockSpec((W, ATT_HEAD_DIM), lambda bi, i: (nxt(i), 0)),
        ],
        out_specs=pl.BlockSpec((1, tq, aw), lambda bi, i: (bi, i, 0)),
        out_shape=jax.ShapeDtypeStruct((b, s, aw), BF16),
        compiler_params=_cparams(("parallel", "parallel")),
        name="attention",
    )(sink, proj, proj, proj, proj, proj, proj, proj, cos2, cos2, cos2, sin2, sin2, sin2)


def _rep2(x):
    return jnp.concatenate([x, x], axis=1)


def _mlstm_body(qf_ref, kf_ref, vf0_ref, vf1_ref, gcf_ref, grf_ref,
                qb_ref, kb_ref, vb0_ref, vb1_ref, gcb_ref, grb_ref,
                hf_ref, hb_ref, st_ref, m_ref, *, chains_per_group):
    @pl.when(pl.program_id(1) == 0)
    def _():
        st_ref[...] = jnp.zeros_like(st_ref)
        m_ref[...] = jnp.zeros_like(m_ref)

    L = qf_ref.shape[1]
    dirs = ((qf_ref, kf_ref, (vf0_ref, vf1_ref), gcf_ref, grf_ref, hf_ref),
            (qb_ref, kb_ref, (vb0_ref, vb1_ref), gcb_ref, grb_ref, hb_ref))
    r = lax.broadcasted_iota(jnp.int32, (L, L), 0)
    c = lax.broadcasted_iota(jnp.int32, (L, L), 1)
    tris = (c <= r, c >= r)
    all_chains = [(rev, h) for rev in range(2) for h in range(M_HEADS)]
    for g0 in range(0, len(all_chains), chains_per_group):
        _mlstm_group(all_chains[g0:g0 + chains_per_group], dirs, tris, st_ref, m_ref)


def _mlstm_group(chains, dirs, tris, st_ref, m_ref):
    D = M_HEAD_DIM
    R = M_NCOL
    L = tris[0].shape[0]
    slot = [rev * M_HEADS + h for rev, h in chains]
    q, k, v, bc, igc, arow, m_prev = [], [], [], [], [], [], []
    for rev, h in chains:
        q_ref, k_ref, v_refs, gc_ref, gr_ref, _ = dirs[rev]
        ig_lane = rev * M_HEADS + h
        b_lane = (2 + rev) * M_HEADS + h
        q.append(q_ref[0, :, h * D:(h + 1) * D])
        k.append(k_ref[0, :, h * D:(h + 1) * D])
        v.append(v_refs[h // 2][0, :, (h % 2) * D:(h % 2 + 1) * D])
        bc.append(jnp.broadcast_to(gc_ref[0, :, b_lane:b_lane + 1], (L, R)))
        igc.append(jnp.broadcast_to(gc_ref[0, :, ig_lane:ig_lane + 1], (L, R)))
        arow.append(gr_ref[0, ig_lane:ig_lane + 1, :] - gr_ref[0, b_lane:b_lane + 1, :])
        m_prev.append(m_ref[rev * M_HEADS + h][0:1, :])

    n = len(chains)
    log_d = [jnp.where(tris[chains[i][0]], _rep2(bc[i]) + arow[i], -jnp.inf) for i in range(n)]
    row_max = [jnp.broadcast_to(jnp.max(log_d[i], axis=-1, keepdims=True), (L, R)) for i in range(n)]
    log_inter = [bc[i] + m_prev[i] for i in range(n)]
    m_t = [jnp.maximum(log_inter[i], row_max[i]) for i in range(n)]
    d_mat = [jnp.exp(log_d[i] - _rep2(m_t[i])) for i in range(n)]
    inter = [jnp.exp(log_inter[i] - m_t[i]) for i in range(n)]
    qk = [lax.dot_general(q[i], k[i], (((1,), (1,)), ((), ())), preferred_element_type=F32) for i in range(n)]
    s = [qk[i] * d_mat[i] for i in range(n)]
    p2 = [jnp.dot(q[i], st_ref[slot[i]].astype(BF16), preferred_element_type=F32) for i in range(n)]
    sv = [jnp.dot(s[i].astype(BF16), v[i], preferred_element_type=F32) for i in range(n)]
    for i, (rev, h) in enumerate(chains):
        num = sv[i] + _rep2(inter[i]) * p2[i][:, :D]
        den = jnp.broadcast_to(jnp.sum(s[i], axis=-1, keepdims=True), (L, R)) + inter[i] * p2[i][:, D:]
        scale = 1.0 / jnp.maximum(jnp.abs(den), jnp.exp(-m_t[i]))
        dirs[rev][5][0, :, h * D:(h + 1) * D] = num * _rep2(scale)

    for i, (rev, h) in enumerate(chains):
        b_last = bc[i][0:1, :] if rev else bc[i][L - 1:L, :]
        log_w = b_last - bc[i] + igc[i]
        m_new = jnp.maximum(b_last + m_prev[i], jnp.max(log_w, axis=0, keepdims=True))
        w = jnp.exp(log_w - m_new)
        decay = jnp.exp(b_last + m_prev[i] - m_new)
        wext = jnp.concatenate([(_rep2(w) * v[i].astype(F32)).astype(BF16), w.astype(BF16)], axis=1)
        upd = lax.dot_general(k[i], wext, (((0,), (0,)), ((), ())), preferred_element_type=F32)
        st_ref[slot[i]] = jnp.concatenate([decay, decay, decay], axis=1) * st_ref[slot[i]] + upd
        m_ref[slot[i]] = jnp.broadcast_to(m_new, m_ref.shape[1:])


def _mlstm(qk, proj, gcol, grow, vcol0, chains_per_group=2):
    b, s, _ = qk.shape
    L = M_CHUNK
    nc = s // L
    mw = M_HEADS * M_HEAD_DIM
    vw = mw // 2
    vb = vcol0 // vw
    fwd = lambda i: i
    bwd = lambda i: nc - 1 - i

    def specs(pos):
        return [
            pl.BlockSpec((1, L, mw), lambda bi, i: (bi, pos(i), 0)),
            pl.BlockSpec((1, L, mw), lambda bi, i: (bi, pos(i), 1)),
            pl.BlockSpec((1, L, vw), lambda bi, i: (bi, pos(i), vb)),
            pl.BlockSpec((1, L, vw), lambda bi, i: (bi, pos(i), vb + 1)),
            pl.BlockSpec((1, L, GATE_LANES), lambda bi, i: (bi, pos(i), 0)),
            pl.BlockSpec((1, N_GATES, L), lambda bi, i: (bi, 0, pos(i))),
        ]

    nchain = 2 * M_HEADS
    return pl.pallas_call(
        functools.partial(_mlstm_body, chains_per_group=chains_per_group),
        grid=(b, nc),
        in_specs=specs(fwd) + specs(bwd),
        out_specs=[
            pl.BlockSpec((1, L, mw), lambda bi, i: (bi, fwd(i), 0)),
            pl.BlockSpec((1, L, mw), lambda bi, i: (bi, bwd(i), 0)),
        ],
        out_shape=[jax.ShapeDtypeStruct((b, s, mw), F32), jax.ShapeDtypeStruct((b, s, mw), F32)],
        scratch_shapes=[
            pltpu.VMEM((nchain, M_HEAD_DIM, M_HEAD_DIM + M_NCOL), F32),
            pltpu.VMEM((nchain, 8, 128), F32),
        ],
        compiler_params=_cparams(("parallel", "arbitrary")),
        name="mlstm",
    )(qk, qk, proj, proj, gcol, grow, qk, qk, proj, proj, gcol, grow)


def _outproj_body(x_ref, ya_ref, hf_ref, hb_ref, o0_ref, o1_ref, mn_ref, w_ref, g_ref, o_ref):
    D = M_HEAD_DIM
    hm = hf_ref[...] + hb_ref[...]
    ys = []
    for h in range(M_HEADS):
        o_gate = (o0_ref, o1_ref)[h // 2][:, (h % 2) * D:(h % 2 + 1) * D].astype(F32)
        y = _rms(hm[:, h * D:(h + 1) * D], mn_ref[:, h * D:(h + 1) * D])
        ys.append((jax.nn.sigmoid(o_gate) * y).astype(BF16))
    ym = jnp.concatenate(ys, axis=-1)
    aw = ya_ref.shape[-1]
    m = jnp.dot(ya_ref[...], w_ref[0:aw, :], preferred_element_type=F32)
    m = m + jnp.dot(ym, w_ref[aw:, :], preferred_element_type=F32)
    o_ref[...] = x_ref[...] + _rms(m, g_ref[...])


def _outproj(x, ya, hf, hb, proj, mnorm, w, g, layer, ocol0, tm=512):
    t, d = x.shape
    aw = ya.shape[-1]
    mw = hf.shape[-1]
    ow = mw // 2
    ob = ocol0 // ow
    return pl.pallas_call(
        _outproj_body,
        grid=(t // tm,),
        in_specs=[
            pl.BlockSpec((tm, d), lambda i: (i, 0)),
            pl.BlockSpec((tm, aw), lambda i: (i, 0)),
            pl.BlockSpec((tm, mw), lambda i: (i, 0)),
            pl.BlockSpec((tm, mw), lambda i: (i, 0)),
            pl.BlockSpec((tm, ow), lambda i: (i, ob)),
            pl.BlockSpec((tm, ow), lambda i: (i, ob + 1)),
            pl.BlockSpec((None, 1, mw), lambda i: (layer, 0, 0)),
            pl.BlockSpec((None, aw + mw, d), lambda i: (layer, 0, 0)),
            pl.BlockSpec((None, 1, d), lambda i: (layer, 0, 0)),
        ],
        out_specs=pl.BlockSpec((tm, d), lambda i: (i, 0)),
        out_shape=jax.ShapeDtypeStruct((t, d), F32),
        compiler_params=_cparams(("parallel",)),
        name="outproj",
    )(x, ya, hf, hb, proj, proj, mnorm, w, g)


def _rope_tables(seq):
    half = ATT_HEAD_DIM // 2
    pos = jnp.arange(seq, dtype=F32)
    inv_freq = ROPE_THETA ** (-jnp.arange(half, dtype=F32) / half)
    ang = pos[:, None] * inv_freq[None, :]
    cos, sin = jnp.cos(ang), jnp.sin(ang)
    return jnp.concatenate([cos, cos], axis=-1), jnp.concatenate([-sin, sin], axis=-1)


def _mixer(x2, b, s, layer, mix_norm_pre, mix_norm_post, w_main, w_gate, b_gate_pad, conv_w,
           attn_sink, mlstm_norm, w_out, cos2, sin2):
    t, d = x2.shape
    att_w = ATT_HEADS * ATT_HEAD_DIM
    kv_w = ATT_KV_HEADS * ATT_HEAD_DIM
    m_w = M_HEADS * M_HEAD_DIM
    qm0 = att_w + 2 * kv_w
    vm0 = qm0 + 2 * m_w
    om0 = vm0 + m_w
    proj, gates = _inproj(x2, mix_norm_pre, w_main, w_gate, layer)
    proj3 = proj.reshape(b, s, proj.shape[-1])
    gcol, grow = _gateprep(gates.reshape(b, s, GATE_LANES), b_gate_pad[layer])
    qk = _conv_silu(proj3, conv_w, layer, qm0)
    ya = _attention(proj3, cos2, sin2, attn_sink[layer])
    hf, hb = _mlstm(qk, proj3, gcol, grow, vm0)
    return _outproj(x2, ya.reshape(t, att_w), hf.reshape(t, m_w), hb.reshape(t, m_w), proj,
                    mlstm_norm, w_out, mix_norm_post, layer, om0)


def kernel(x, ffn1_norm_pre, ffn1_norm_post, ffn1_w_gate, ffn1_w_up, ffn1_w_down, mix_norm_pre, mix_norm_post, w_in, b_gate, conv_w, attn_sink, mlstm_norm, w_out, ffn2_norm_pre, ffn2_norm_post, ffn2_w_gate, ffn2_w_up, ffn2_w_down):
    b, s, d = x.shape
    depth = w_in.shape[0]
    n_main = w_in.shape[-1] - N_GATES
    vec = lambda g: g.reshape(depth, 1, g.shape[-1])
    bf = lambda w: w.astype(BF16)
    w_main = bf(w_in[:, :, :n_main])
    w_gate = bf(jnp.pad(w_in[:, :, n_main:], ((0, 0), (0, 0), (0, GATE_LANES - N_GATES))))
    b_gate_pad = jnp.pad(b_gate, ((0, 0), (0, GATE_LANES - N_GATES))).reshape(depth, 1, GATE_LANES)
    ffn_norms = ((vec(ffn1_norm_pre), vec(ffn1_norm_post)), (vec(ffn2_norm_pre), vec(ffn2_norm_post)))
    ffn_w32 = ((ffn1_w_gate, ffn1_w_up, ffn1_w_down), (ffn2_w_gate, ffn2_w_up, ffn2_w_down))
    w_out_b = bf(w_out)
    cos2, sin2 = _rope_tables(s)
    x2 = x.reshape(b * s, d)
    ffn_w = tuple(bf(w[0]) for w in ffn_w32[0])
    for layer in range(depth):
        x2, ffn_w = _ffn(x2, *ffn_norms[0], *ffn_w, layer, side=(ffn_w32[1], layer))
        x2 = _mixer(x2, b, s, layer, vec(mix_norm_pre), vec(mix_norm_post), w_main, w_gate, b_gate_pad,
                    conv_w, attn_sink, vec(mlstm_norm), w_out_b, cos2, sin2)
        side = (ffn_w32[0], layer + 1) if layer + 1 < depth else None
        x2, ffn_w = _ffn(x2, *ffn_norms[1], *ffn_w, layer, side=side)
    return x2.reshape(b, s, d)
```

```python
import functools

import jax
import jax.numpy as jnp
from jax import lax
from jax.experimental import pallas as pl
from jax.experimental.pallas import tpu as pltpu

F32 = jnp.float32
BF16 = jnp.bfloat16

EPS = 1e-6
ROPE_THETA = 10000.0
ATT_HEADS = 8
ATT_KV_HEADS = 2
ATT_GROUP = ATT_HEADS // ATT_KV_HEADS
ATT_HEAD_DIM = 128
ATT_BLOCK = 128
M_HEADS = 4
M_HEAD_DIM = 256
M_CHUNK = 256
M_NCOL = 128
CONV_WIDTH = 5
CONV_HALO = 16
CONV_ROW_BLOCK = 128
CONV_WINDOW = 256
GATE_LANES = 128
N_GATES = 4 * M_HEADS

V7X_VMEM_BYTES = 64 * 1024 * 1024
VMEM_LIMIT = 56 * 1024 * 1024


def _cparams(sem):
    return pltpu.CompilerParams(dimension_semantics=sem, vmem_limit_bytes=VMEM_LIMIT)


def _rms(x, g):
    return x * lax.rsqrt(jnp.mean(x * x, axis=-1, keepdims=True) + EPS) * g


def _ffn_body(xe_ref, xn_ref, gpre_ref, gpost_ref, wg_ref, wu_ref, wd_ref, *rest, n_chunks, n_side):
    side_in, (o_ref, *side_out), (h_ref, acc_ref) = rest[:n_side], rest[n_side:2 * n_side + 1], rest[2 * n_side + 1:]
    r, j = pl.program_id(0), pl.program_id(1)
    last_r = pl.num_programs(0) - 1
    par = r % 2
    oth = 1 - par
    rc = xn_ref.shape[0]
    rows = pl.ds(pl.multiple_of(jnp.minimum(j, n_chunks - 1) * rc, rc), rc)
    g_pre = gpre_ref[...]
    g_post_half = 0.5 * gpost_ref[...]

    def prenorm_chunk():
        return _rms(xn_ref[...], g_pre).astype(BF16)

    def postnorm_chunk():
        return xe_ref[...] + _rms(acc_ref[par, rows, :], g_post_half)

    @pl.when(r == 0)
    def _():
        @pl.when(j == 0)
        def _():
            acc_ref[...] = jnp.zeros_like(acc_ref)
        h_ref[par, rows, :] = prenorm_chunk()

    def matmul_step(with_chunk):
        if with_chunk:
            out_chunk = postnorm_chunk()
            h_chunk = prenorm_chunk()
        h = h_ref[oth]
        g = jnp.dot(h, wg_ref[...], preferred_element_type=F32)
        u = jnp.dot(h, wu_ref[...], preferred_element_type=F32)
        a = (g * jax.nn.sigmoid(g) * u).astype(BF16)
        acc_ref[oth] = jnp.where(j == 0, 0.0, acc_ref[oth]) + jnp.dot(a, wd_ref[...], preferred_element_type=F32)
        if with_chunk:
            o_ref[...] = out_chunk
            h_ref[par, rows, :] = h_chunk
        for src_ref, dst_ref in zip(side_in, side_out):
            dst_ref[...] = src_ref[...].astype(BF16)

    matmul_row = (r > 0) & (r < last_r)
    pl.when(matmul_row & (j < n_chunks))(functools.partial(matmul_step, True))
    pl.when(matmul_row & (j >= n_chunks))(functools.partial(matmul_step, False))

    @pl.when(r == last_r)
    def _():
        o_ref[...] = postnorm_chunk()


def _side_rows(n_rows, max_blocks):
    rb = 16
    while n_rows % rb or n_rows // rb > max_blocks:
        rb += 16
    return rb


def _ffn(x, gpre, gpost, wg, wu, wd, layer, side=None, tm=1024, tf=512, rows_per_step=128):
    t, d = x.shape
    f = wg.shape[-1]
    tm = min(tm, t)
    nt, nj = t // tm, f // tf
    n_chunks = tm // rows_per_step
    assert n_chunks <= nj, "every row chunk of a tile needs its own d_ff step"
    n_blocks = nt * n_chunks
    chunk = lambda j: jnp.minimum(j, n_chunks - 1)
    emit_map = lambda r, j: (jnp.clip((r - 2) * n_chunks + chunk(j), 0, n_blocks - 1), 0)
    next_map = lambda r, j: (jnp.minimum(r * n_chunks + chunk(j), n_blocks - 1), 0)
    wj = lambda r, j: jnp.where(r == 0, 0, jnp.where(r == nt + 1, nj - 1, j))

    side_w, side_layer = side if side is not None else ((), 0)
    side_in_specs, side_out_specs, side_shapes = [], [], []
    for w in side_w:
        _, n_rows, n_cols = w.shape
        rb = _side_rows(n_rows, nt * nj)
        blk = lambda r, j, nb=n_rows // rb: jnp.clip((r - 1) * nj + j, 0, nb - 1)
        side_in_specs.append(pl.BlockSpec((None, rb, n_cols), lambda r, j, blk=blk: (side_layer, blk(r, j), 0)))
        side_out_specs.append(pl.BlockSpec((rb, n_cols), lambda r, j, blk=blk: (blk(r, j), 0)))
        side_shapes.append(jax.ShapeDtypeStruct((n_rows, n_cols), BF16))

    out, *side_out = pl.pallas_call(
        functools.partial(_ffn_body, n_chunks=n_chunks, n_side=len(side_w)),
        grid=(nt + 2, nj),
        in_specs=[
            pl.BlockSpec((rows_per_step, d), emit_map),
            pl.BlockSpec((rows_per_step, d), next_map),
            pl.BlockSpec((None, 1, d), lambda r, j: (layer, 0, 0)),
            pl.BlockSpec((None, 1, d), lambda r, j: (layer, 0, 0)),
            pl.BlockSpec((d, tf), lambda r, j: (0, wj(r, j))),
            pl.BlockSpec((d, tf), lambda r, j: (0, wj(r, j))),
            pl.BlockSpec((tf, d), lambda r, j: (wj(r, j), 0)),
        ] + side_in_specs,
        out_specs=[pl.BlockSpec((rows_per_step, d), emit_map)] + side_out_specs,
        out_shape=[jax.ShapeDtypeStruct((t, d), F32)] + side_shapes,
        scratch_shapes=[pltpu.VMEM((2, tm, d), BF16), pltpu.VMEM((2, tm, d), F32)],
        compiler_params=_cparams(("arbitrary", "arbitrary")),
        name="ffn",
    )(x, x, gpre, gpost, wg, wu, wd, *side_w)
    return out, tuple(side_out)


def _inproj_body(x_ref, g_ref, w_ref, wgate_ref, o_ref, og_ref, h_ref):
    r = pl.program_id(0)
    j = pl.program_id(1)
    par = r % 2
    rc = x_ref.shape[0]
    rows = pl.ds(pl.multiple_of(j * rc, rc), rc)
    h_chunk = _rms(x_ref[...], g_ref[...]).astype(BF16)
    og_ref[...] = jnp.dot(h_chunk, wgate_ref[...], preferred_element_type=F32)

    @pl.when(r > 0)
    def _():
        o_ref[...] = jnp.dot(h_ref[1 - par], w_ref[...], preferred_element_type=F32).astype(BF16)

    h_ref[par, rows, :] = h_chunk


def _inproj(x, g, w, wgate, layer, n, tm=512, tn=2816):
    t, d = x.shape
    tm = min(tm, t)
    nt, nj = t // tm, n // tn
    rc = tm // nj
    chunk_map = lambda r, j: (jnp.minimum(r * nj + j, nt * nj - 1), 0)
    return pl.pallas_call(
        _inproj_body,
        grid=(nt + 1, nj),
        in_specs=[
            pl.BlockSpec((rc, d), chunk_map),
            pl.BlockSpec((None, 1, d), lambda r, j: (layer, 0, 0)),
            pl.BlockSpec((None, d, tn), lambda r, j: (layer, 0, j)),
            pl.BlockSpec((None, d, GATE_LANES), lambda r, j: (layer, 0, 0)),
        ],
        out_specs=[
            pl.BlockSpec((tm, tn), lambda r, j: (jnp.maximum(r - 1, 0), jnp.where(r == 0, 0, j))),
            pl.BlockSpec((rc, GATE_LANES), chunk_map),
        ],
        out_shape=[
            jax.ShapeDtypeStruct((t, n), BF16),
            jax.ShapeDtypeStruct((t, GATE_LANES), F32),
        ],
        scratch_shapes=[pltpu.VMEM((2, tm, d), BF16)],
        compiler_params=_cparams(("arbitrary", "arbitrary")),
        name="inproj",
    )(x, g, w, wgate)


def _log_sigmoid(x):
    return -(jnp.maximum(-x, 0.0) + jnp.log1p(jnp.exp(-jnp.abs(x))))


def _split3(x):
    hi = x.astype(BF16)
    r1 = x - hi.astype(F32)
    mid = r1.astype(BF16)
    lo = (r1 - mid.astype(F32)).astype(BF16)
    return hi, mid, lo


def _gateprep_body(g_ref, bias_ref, gcol_ref, grow_ref, *, nch):
    L = M_CHUNK
    r = lax.broadcasted_iota(jnp.int32, (2 * L, L), 0)
    c = lax.broadcasted_iota(jnp.int32, (2 * L, L), 1)
    tri = jnp.where(((r < L) & (c <= r)) | ((r >= L) & (c >= r - L)), 1.0, 0.0).astype(BF16)
    lane = lax.broadcasted_iota(jnp.int32, (L, GATE_LANES), 1)
    for ch in range(nch):
        x = g_ref[0, ch * L:(ch + 1) * L, :] + bias_ref[...]
        cs = None
        for part in _split3(_log_sigmoid(x)):
            d = jnp.dot(tri, part, preferred_element_type=F32)
            cs = d if cs is None else cs + d
        out = jnp.where(lane < 2 * M_HEADS, x, jnp.where(lane < 3 * M_HEADS, cs[:L], cs[L:]))
        gcol_ref[0, ch * L:(ch + 1) * L, :] = out
        grow_ref[0, :, ch * L:(ch + 1) * L] = out.T[0:N_GATES, :]


def _gateprep(gates, bias, rows=1024):
    b, s, _ = gates.shape
    rows = min(rows, s)
    return pl.pallas_call(
        functools.partial(_gateprep_body, nch=rows // M_CHUNK),
        grid=(b, s // rows),
        in_specs=[
            pl.BlockSpec((1, rows, GATE_LANES), lambda i, j: (i, j, 0)),
            pl.BlockSpec((1, GATE_LANES), lambda i, j: (0, 0)),
        ],
        out_specs=[
            pl.BlockSpec((1, rows, GATE_LANES), lambda i, j: (i, j, 0)),
            pl.BlockSpec((1, N_GATES, rows), lambda i, j: (i, 0, j)),
        ],
        out_shape=[
            jax.ShapeDtypeStruct((b, s, GATE_LANES), F32),
            jax.ShapeDtypeStruct((b, N_GATES, s), F32),
        ],
        compiler_params=_cparams(("parallel", "parallel")),
        name="gateprep",
    )(gates, bias)


def _conv_body(xp_ref, xc_ref, xn_ref, w_ref, o_ref, xe_ref, *, ts):
    i = pl.program_id(1)
    c = pl.program_id(2)
    H, RB, WIN = CONV_HALO, CONV_ROW_BLOCK, CONV_WINDOW
    pad = CONV_WIDTH // 2
    tc = xc_ref.shape[-1]
    xe_ref[0:H, :] = jnp.where(i == 0, 0.0, xp_ref[0]).astype(BF16)
    xe_ref[H:H + ts, :] = xc_ref[0]
    xe_ref[H + ts:H + ts + H, :] = jnp.where(i == pl.num_programs(1) - 1, 0.0, xn_ref[0]).astype(BF16)
    xe_ref[H + ts + H:, :] = jnp.zeros((xe_ref.shape[0] - (ts + 2 * H), tc), BF16)

    taps = [k for k in range(CONV_WIDTH) if k != pad]
    diag = (lax.broadcasted_iota(jnp.int32, (RB, WIN), 1) - lax.broadcasted_iota(jnp.int32, (RB, WIN), 0))
    shift = jnp.concatenate([jnp.where(diag == H - pad + k, 1.0, 0.0).astype(BF16) for k in taps], axis=0)

    scale = jnp.where(c >= pl.num_programs(2) // 2, M_HEAD_DIM ** -0.5, 1.0)
    for b in range(ts // RB):
        window = xe_ref[b * RB:b * RB + WIN, :]
        shifted = jnp.dot(shift, window, preferred_element_type=F32)
        acc = xc_ref[0, b * RB:(b + 1) * RB, :].astype(F32) * w_ref[pad:pad + 1, :]
        for t, k in enumerate(taps):
            acc = acc + shifted[t * RB:(t + 1) * RB, :] * w_ref[k:k + 1, :]
        y = acc * jax.nn.sigmoid(acc)
        o_ref[0, b * RB:(b + 1) * RB, :] = (y * scale).astype(BF16)


def _conv_silu(proj, conv_w, layer, col0, ts=1024, tc=512):
    b, s, _ = proj.shape
    ts = min(ts, s)
    width = conv_w.shape[-1]
    cb0 = col0 // tc
    hb = ts // CONV_HALO
    nhb = s // CONV_HALO
    assert CONV_WINDOW >= CONV_ROW_BLOCK + 2 * CONV_HALO and ts % CONV_ROW_BLOCK == 0
    return pl.pallas_call(
        functools.partial(_conv_body, ts=ts),
        grid=(b, s // ts, width // tc),
        in_specs=[
            pl.BlockSpec((1, CONV_HALO, tc), lambda bi, i, c: (bi, jnp.maximum(i * hb - 1, 0), cb0 + c)),
            pl.BlockSpec((1, ts, tc), lambda bi, i, c: (bi, i, cb0 + c)),
            pl.BlockSpec((1, CONV_HALO, tc), lambda bi, i, c: (bi, jnp.minimum((i + 1) * hb, nhb - 1), cb0 + c)),
            pl.BlockSpec((None, CONV_WIDTH, tc), lambda bi, i, c: (layer, 0, c)),
        ],
        out_specs=pl.BlockSpec((1, ts, tc), lambda bi, i, c: (bi, i, c)),
        out_shape=jax.ShapeDtypeStruct((b, s, width), BF16),
        scratch_shapes=[pltpu.VMEM((ts - CONV_ROW_BLOCK + CONV_WINDOW, tc), BF16)],
        compiler_params=_cparams(("parallel", "parallel", "parallel")),
        name="conv_silu",
    )(proj, proj, proj, conv_w)


def _rope(x_bf16, cos, sin_signed):
    n = x_bf16.shape[-1] // ATT_HEAD_DIM
    outs = []
    for h in range(n):
        x = x_bf16[:, h * ATT_HEAD_DIM:(h + 1) * ATT_HEAD_DIM].astype(F32)
        y = x * cos + pltpu.roll(x, ATT_HEAD_DIM // 2, axis=1) * sin_signed
        outs.append(y.astype(BF16))
    return outs


def _attn_body(sink_ref, q_ref, kp_ref, kc_ref, kn_ref, vp_ref, vc_ref, vn_ref,
               cp_ref, cc_ref, cn_ref, sp_ref, sc_ref, sn_ref, o_ref, *, tq):
    i = pl.program_id(1)
    W = ATT_BLOCK
    nsb = tq // W
    qh = _rope(q_ref[0], cc_ref[...], sc_ref[...])
    kp = _rope(kp_ref[0], cp_ref[...], sp_ref[...])
    kc = _rope(kc_ref[0], cc_ref[...], sc_ref[...])
    kn = _rope(kn_ref[0], cn_ref[...], sn_ref[...])
    kwin = [jnp.concatenate([kp[h], kc[h], kn[h]], axis=0) for h in range(ATT_KV_HEADS)]
    vall = jnp.concatenate([vp_ref[0], vc_ref[0], vn_ref[0]], axis=0)
    r = lax.broadcasted_iota(jnp.int32, (W, W), 0)
    c = lax.broadcasted_iota(jnp.int32, (W, W), 1)
    first_block = i == 0
    last_block = i == pl.num_programs(1) - 1
    scale = ATT_HEAD_DIM ** -0.5
    units = [(sb, hk) for sb in range(nsb) for hk in range(ATT_KV_HEADS)]

    def masks(sb):
        left = c >= r
        right = c <= r
        if sb == 0:
            left = left & jnp.logical_not(first_block)
        if sb == nsb - 1:
            right = right & jnp.logical_not(last_block)
        return left, right

    scores = []
    for sb, hk in units:
        q4 = jnp.concatenate(
            [qh[hk * ATT_GROUP + g][sb * W:(sb + 1) * W, :] for g in range(ATT_GROUP)], axis=0)
        kw = kwin[hk][sb * W:(sb + 3) * W, :]
        scores.append(lax.dot_general(q4, kw, (((1,), (1,)), ((), ())), preferred_element_type=F32) * scale)

    probs, dens = [], []
    for (sb, hk), s in zip(units, scores):
        left, right = masks(sb)
        ps, ds = [], []
        for g in range(ATT_GROUP):
            sink = sink_ref[hk * ATT_GROUP + g]
            sg = s[g * W:(g + 1) * W, :]
            sl = jnp.where(left, sg[:, :W], -jnp.inf)
            sm = sg[:, W:2 * W]
            sr = jnp.where(right, sg[:, 2 * W:], -jnp.inf)
            m = jnp.maximum(jnp.max(jnp.maximum(jnp.maximum(sl, sm), sr), axis=-1, keepdims=True), sink)
            pl_, pm, pr = jnp.exp(sl - m), jnp.exp(sm - m), jnp.exp(sr - m)
            ds.append(jnp.sum(pl_ + pm + pr, axis=-1, keepdims=True) + jnp.exp(sink - m))
            ps.append(jnp.concatenate([pl_, pm, pr], axis=1).astype(BF16))
        probs.append(jnp.concatenate(ps, axis=0))
        dens.append(ds)

    outs = []
    for (sb, hk), p in zip(units, probs):
        vw = vall[sb * W:(sb + 3) * W, hk * ATT_HEAD_DIM:(hk + 1) * ATT_HEAD_DIM]
        outs.append(jnp.dot(p, vw, preferred_element_type=F32))

    for (sb, hk), pv, ds in zip(units, outs, dens):
        for g in range(ATT_GROUP):
            h = hk * ATT_GROUP + g
            o_ref[0, sb * W:(sb + 1) * W, h * ATT_HEAD_DIM:(h + 1) * ATT_HEAD_DIM] = (
                pv[g * W:(g + 1) * W, :] / ds[g]).astype(BF16)


def _attention(proj, cos2, sin2, sink, tq=512):
    b, s, _ = proj.shape
    tq = min(tq, s)
    W = ATT_BLOCK
    aw = ATT_HEADS * ATT_HEAD_DIM
    kvw = ATT_KV_HEADS * ATT_HEAD_DIM
    kb, vb = aw // kvw, aw // kvw + 1
    nb = tq // W
    nwb = s // W
    prev = lambda i: jnp.maximum(i * nb - 1, 0)
    nxt = lambda i: jnp.minimum((i + 1) * nb, nwb - 1)
    return pl.pallas_call(
        functools.partial(_attn_body, tq=tq),
        grid=(b, s // tq),
        in_specs=[
            pl.BlockSpec(memory_space=pltpu.SMEM),
            pl.BlockSpec((1, tq, aw), lambda bi, i: (bi, i, 0)),
            pl.BlockSpec((1, W, kvw), lambda bi, i: (bi, prev(i), kb)),
            pl.BlockSpec((1, tq, kvw), lambda bi, i: (bi, i, kb)),
            pl.BlockSpec((1, W, kvw), lambda bi, i: (bi, nxt(i), kb)),
            pl.BlockSpec((1, W, kvw), lambda bi, i: (bi, prev(i), vb)),
            pl.BlockSpec((1, tq, kvw), lambda bi, i: (bi, i, vb)),
            pl.BlockSpec((1, W, kvw), lambda bi, i: (bi, nxt(i), vb)),
            pl.BlockSpec((W, ATT_HEAD_DIM), lambda bi, i: (prev(i), 0)),
            pl.BlockSpec((tq, ATT_HEAD_DIM), lambda bi, i: (i, 0)),
            pl.BlockSpec((W, ATT_HEAD_DIM), lambda bi, i: (nxt(i), 0)),
            pl.BlockSpec((W, ATT_HEAD_DIM), lambda bi, i: (prev(i), 0)),
            pl.BlockSpec((tq, ATT_HEAD_DIM), lambda bi, i: (i, 0)),
            pl.BlockSpec((W, ATT_HEAD_DIM), lambda bi, i: (nxt(i), 0)),
        ],
        out_specs=pl.BlockSpec((1, tq, aw), lambda bi, i: (bi, i, 0)),
        out_shape=jax.ShapeDtypeStruct((b, s, aw), BF16),
        compiler_params=_cparams(("parallel", "parallel")),
        name="attention",
    )(sink, proj, proj, proj, proj, proj, proj, proj, cos2, cos2, cos2, sin2, sin2, sin2)


def _rep2(x):
    return jnp.concatenate([x, x], axis=1)


def _mlstm_body(qf_ref, kf_ref, vf0_ref, vf1_ref, gcf_ref, grf_ref,
                qb_ref, kb_ref, vb0_ref, vb1_ref, gcb_ref, grb_ref,
                hf_ref, hb_ref, st_ref, m_ref, *, chains_per_group):
    @pl.when(pl.program_id(1) == 0)
    def _():
        st_ref[...] = jnp.zeros_like(st_ref)
        m_ref[...] = jnp.zeros_like(m_ref)

    L = M_CHUNK
    n_sub = qf_ref.shape[1] // L
    dirs = ((qf_ref, kf_ref, (vf0_ref, vf1_ref), gcf_ref, grf_ref, hf_ref),
            (qb_ref, kb_ref, (vb0_ref, vb1_ref), gcb_ref, grb_ref, hb_ref))
    r = lax.broadcasted_iota(jnp.int32, (L, L), 0)
    c = lax.broadcasted_iota(jnp.int32, (L, L), 1)
    tris = (c <= r, c >= r)
    all_chains = [(rev, h) for rev in range(2) for h in range(M_HEADS)]
    for sub in range(n_sub):
        rows = (slice(sub * L, (sub + 1) * L), slice((n_sub - 1 - sub) * L, (n_sub - sub) * L))
        for g0 in range(0, len(all_chains), chains_per_group):
            _mlstm_group(all_chains[g0:g0 + chains_per_group], dirs, rows, tris, st_ref, m_ref)


def _mlstm_group(chains, dirs, rows, tris, st_ref, m_ref):
    D = M_HEAD_DIM
    R = M_NCOL
    L = tris[0].shape[0]
    slot = [rev * M_HEADS + h for rev, h in chains]
    q, k, v, bc, igc, arow, m_prev = [], [], [], [], [], [], []
    for rev, h in chains:
        q_ref, k_ref, v_refs, gc_ref, gr_ref, _ = dirs[rev]
        rs = rows[rev]
        ig_lane = rev * M_HEADS + h
        b_lane = (2 + rev) * M_HEADS + h
        q.append(q_ref[0, rs, h * D:(h + 1) * D])
        k.append(k_ref[0, rs, h * D:(h + 1) * D])
        v.append(v_refs[h // 2][0, rs, (h % 2) * D:(h % 2 + 1) * D])
        bc.append(jnp.broadcast_to(gc_ref[0, rs, b_lane:b_lane + 1], (L, R)))
        igc.append(jnp.broadcast_to(gc_ref[0, rs, ig_lane:ig_lane + 1], (L, R)))
        arow.append(gr_ref[0, ig_lane:ig_lane + 1, rs] - gr_ref[0, b_lane:b_lane + 1, rs])
        m_prev.append(m_ref[rev * M_HEADS + h][0:1, :])

    n = len(chains)
    log_d = [jnp.where(tris[chains[i][0]], _rep2(bc[i]) + arow[i], -jnp.inf) for i in range(n)]
    row_max = [jnp.broadcast_to(jnp.max(log_d[i], axis=-1, keepdims=True), (L, R)) for i in range(n)]
    log_inter = [bc[i] + m_prev[i] for i in range(n)]
    m_t = [jnp.maximum(log_inter[i], row_max[i]) for i in range(n)]
    d_mat = [jnp.exp(log_d[i] - _rep2(m_t[i])) for i in range(n)]
    inter = [jnp.exp(log_inter[i] - m_t[i]) for i in range(n)]
    qk = [lax.dot_general(q[i], k[i], (((1,), (1,)), ((), ())), preferred_element_type=F32) for i in range(n)]
    s = [qk[i] * d_mat[i] for i in range(n)]
    p2 = [jnp.dot(q[i], st_ref[slot[i]].astype(BF16), preferred_element_type=F32) for i in range(n)]
    sv = [jnp.dot(s[i].astype(BF16), v[i], preferred_element_type=F32) for i in range(n)]
    for i, (rev, h) in enumerate(chains):
        num = sv[i] + _rep2(inter[i]) * p2[i][:, :D]
        den = jnp.broadcast_to(jnp.sum(s[i], axis=-1, keepdims=True), (L, R)) + inter[i] * p2[i][:, D:]
        scale = 1.0 / jnp.maximum(jnp.abs(den), jnp.exp(-m_t[i]))
        dirs[rev][5][0, rows[rev], h * D:(h + 1) * D] = num * _rep2(scale)

    for i, (rev, h) in enumerate(chains):
        b_last = bc[i][0:1, :] if rev else bc[i][L - 1:L, :]
        log_w = b_last - bc[i] + igc[i]
        m_new = jnp.maximum(b_last + m_prev[i], jnp.max(log_w, axis=0, keepdims=True))
        w = jnp.exp(log_w - m_new)
        decay = jnp.exp(b_last + m_prev[i] - m_new)
        wext = jnp.concatenate([(_rep2(w) * v[i].astype(F32)).astype(BF16), w.astype(BF16)], axis=1)
        upd = lax.dot_general(k[i], wext, (((0,), (0,)), ((), ())), preferred_element_type=F32)
        st_ref[slot[i]] = jnp.concatenate([decay, decay, decay], axis=1) * st_ref[slot[i]] + upd
        m_ref[slot[i]] = jnp.broadcast_to(m_new, m_ref.shape[1:])


def _mlstm(qk, proj, gcol, grow, vcol0, chains_per_group=2, chunks_per_step=2):
    b, s, _ = qk.shape
    L = M_CHUNK * chunks_per_step
    nc = s // L
    mw = M_HEADS * M_HEAD_DIM
    vw = mw // 2
    vb = vcol0 // vw
    fwd = lambda i: i
    bwd = lambda i: nc - 1 - i

    def specs(pos):
        return [
            pl.BlockSpec((1, L, mw), lambda bi, i: (bi, pos(i), 0)),
            pl.BlockSpec((1, L, mw), lambda bi, i: (bi, pos(i), 1)),
            pl.BlockSpec((1, L, vw), lambda bi, i: (bi, pos(i), vb)),
            pl.BlockSpec((1, L, vw), lambda bi, i: (bi, pos(i), vb + 1)),
            pl.BlockSpec((1, L, GATE_LANES), lambda bi, i: (bi, pos(i), 0)),
            pl.BlockSpec((1, N_GATES, L), lambda bi, i: (bi, 0, pos(i))),
        ]

    nchain = 2 * M_HEADS
    return pl.pallas_call(
        functools.partial(_mlstm_body, chains_per_group=chains_per_group),
        grid=(b, nc),
        in_specs=specs(fwd) + specs(bwd),
        out_specs=[
            pl.BlockSpec((1, L, mw), lambda bi, i: (bi, fwd(i), 0)),
            pl.BlockSpec((1, L, mw), lambda bi, i: (bi, bwd(i), 0)),
        ],
        out_shape=[jax.ShapeDtypeStruct((b, s, mw), F32), jax.ShapeDtypeStruct((b, s, mw), F32)],
        scratch_shapes=[
            pltpu.VMEM((nchain, M_HEAD_DIM, M_HEAD_DIM + M_NCOL), F32),
            pltpu.VMEM((nchain, 8, 128), F32),
        ],
        compiler_params=_cparams(("parallel", "arbitrary")),
        name="mlstm",
    )(qk, qk, proj, proj, gcol, grow, qk, qk, proj, proj, gcol, grow)


def _outproj_body(x_ref, ya_ref, hf_ref, hb_ref, o0_ref, o1_ref, mn_ref, w_ref, g_ref, o_ref):
    D = M_HEAD_DIM
    hm = hf_ref[...] + hb_ref[...]
    ys = []
    for h in range(M_HEADS):
        o_gate = (o0_ref, o1_ref)[h // 2][:, (h % 2) * D:(h % 2 + 1) * D].astype(F32)
        y = _rms(hm[:, h * D:(h + 1) * D], mn_ref[:, h * D:(h + 1) * D])
        ys.append((jax.nn.sigmoid(o_gate) * y).astype(BF16))
    ym = jnp.concatenate(ys, axis=-1)
    aw = ya_ref.shape[-1]
    m = jnp.dot(ya_ref[...], w_ref[0:aw, :], preferred_element_type=F32)
    m = m + jnp.dot(ym, w_ref[aw:, :], preferred_element_type=F32)
    o_ref[...] = x_ref[...] + _rms(m, g_ref[...])


def _outproj(x, ya, hf, hb, proj, mnorm, w, g, layer, ocol0, tm=512):
    t, d = x.shape
    aw = ya.shape[-1]
    mw = hf.shape[-1]
    ow = mw // 2
    ob = ocol0 // ow
    return pl.pallas_call(
        _outproj_body,
        grid=(t // tm,),
        in_specs=[
            pl.BlockSpec((tm, d), lambda i: (i, 0)),
            pl.BlockSpec((tm, aw), lambda i: (i, 0)),
            pl.BlockSpec((tm, mw), lambda i: (i, 0)),
            pl.BlockSpec((tm, mw), lambda i: (i, 0)),
            pl.BlockSpec((tm, ow), lambda i: (i, ob)),
            pl.BlockSpec((tm, ow), lambda i: (i, ob + 1)),
            pl.BlockSpec((None, 1, mw), lambda i: (layer, 0, 0)),
            pl.BlockSpec((None, aw + mw, d), lambda i: (layer, 0, 0)),
            pl.BlockSpec((None, 1, d), lambda i: (layer, 0, 0)),
        ],
        out_specs=pl.BlockSpec((tm, d), lambda i: (i, 0)),
        out_shape=jax.ShapeDtypeStruct((t, d), F32),
        compiler_params=_cparams(("parallel",)),
        name="outproj",
    )(x, ya, hf, hb, proj, proj, mnorm, w, g)


def _rope_tables(seq):
    half = ATT_HEAD_DIM // 2
    pos = jnp.arange(seq, dtype=F32)
    inv_freq = ROPE_THETA ** (-jnp.arange(half, dtype=F32) / half)
    ang = pos[:, None] * inv_freq[None, :]
    cos, sin = jnp.cos(ang), jnp.sin(ang)
    return jnp.concatenate([cos, cos], axis=-1), jnp.concatenate([-sin, sin], axis=-1)


def _mixer(x2, b, s, layer, mix_norm_pre, mix_norm_post, w_main, w_gate, b_gate_pad, conv_w,
           attn_sink, mlstm_norm, w_out, cos2, sin2):
    t, d = x2.shape
    att_w = ATT_HEADS * ATT_HEAD_DIM
    kv_w = ATT_KV_HEADS * ATT_HEAD_DIM
    m_w = M_HEADS * M_HEAD_DIM
    qm0 = att_w + 2 * kv_w
    vm0 = qm0 + 2 * m_w
    om0 = vm0 + m_w
    proj, gates = _inproj(x2, mix_norm_pre, w_main, w_gate, layer, qm0 + 4 * m_w)
    proj3 = proj.reshape(b, s, proj.shape[-1])
    gcol, grow = _gateprep(gates.reshape(b, s, GATE_LANES), b_gate_pad[layer])
    qk = _conv_silu(proj3, conv_w, layer, qm0)
    ya = _attention(proj3, cos2, sin2, attn_sink[layer])
    hf, hb = _mlstm(qk, proj3, gcol, grow, vm0)
    return _outproj(x2, ya.reshape(t, att_w), hf.reshape(t, m_w), hb.reshape(t, m_w), proj,
                    mlstm_norm, w_out, mix_norm_post, layer, om0)


def kernel(x, ffn1_norm_pre, ffn1_norm_post, ffn1_w_gate, ffn1_w_up, ffn1_w_down, mix_norm_pre, mix_norm_post, w_in, b_gate, conv_w, attn_sink, mlstm_norm, w_out, ffn2_norm_pre, ffn2_norm_post, ffn2_w_gate, ffn2_w_up, ffn2_w_down):
    b, s, d = x.shape
    depth = w_in.shape[0]
    n_main = w_in.shape[-1] - N_GATES
    vec = lambda g: g.reshape(depth, 1, g.shape[-1])
    bf = lambda w: w.astype(BF16)
    w_main = bf(w_in)
    w_gate = bf(jnp.pad(w_in[:, :, n_main:], ((0, 0), (0, 0), (0, GATE_LANES - N_GATES))))
    b_gate_pad = jnp.pad(b_gate, ((0, 0), (0, GATE_LANES - N_GATES))).reshape(depth, 1, GATE_LANES)
    ffn_norms = ((vec(ffn1_norm_pre), vec(ffn1_norm_post)), (vec(ffn2_norm_pre), vec(ffn2_norm_post)))
    ffn_w32 = ((ffn1_w_gate, ffn1_w_up, ffn1_w_down), (ffn2_w_gate, ffn2_w_up, ffn2_w_down))
    w_out_b = bf(w_out)
    cos2, sin2 = _rope_tables(s)
    x2 = x.reshape(b * s, d)
    ffn_w = tuple(bf(w[0]) for w in ffn_w32[0])
    for layer in range(depth):
        x2, ffn_w = _ffn(x2, *ffn_norms[0], *ffn_w, layer, side=(ffn_w32[1], layer))
        x2 = _mixer(x2, b, s, layer, vec(mix_norm_pre), vec(mix_norm_post), w_main, w_gate, b_gate_pad,
                    conv_w, attn_sink, vec(mlstm_norm), w_out_b, cos2, sin2)
        side = (ffn_w32[0], layer + 1) if layer + 1 < depth else None
        x2, ffn_w = _ffn(x2, *ffn_norms[1], *ffn_w, layer, side=side)
    return x2.reshape(b, s, d)
```

```python
import functools

import jax
import jax.numpy as jnp
from jax import lax
from jax.experimental import pallas as pl
from jax.experimental.pallas import tpu as pltpu

F32 = jnp.float32
BF16 = jnp.bfloat16

EPS = 1e-6
LOG2_E = 1.4426950408889634
ROPE_THETA = 10000.0
ATT_HEADS = 8
ATT_KV_HEADS = 2
ATT_GROUP = ATT_HEADS // ATT_KV_HEADS
ATT_HEAD_DIM = 128
ATT_BLOCK = 128
M_HEADS = 4
M_HEAD_DIM = 256
M_CHUNK = 256
M_NCOL = 128
CONV_WIDTH = 5
CONV_HALO = 16
CONV_ROW_BLOCK = 128
CONV_WINDOW = 256
GATE_LANES = 128
N_GATES = 4 * M_HEADS

V7X_VMEM_BYTES = 64 * 1024 * 1024
VMEM_LIMIT = 56 * 1024 * 1024


def _cparams(sem):
    return pltpu.CompilerParams(dimension_semantics=sem, vmem_limit_bytes=VMEM_LIMIT)


def _rms(x, g):
    return x * lax.rsqrt(jnp.mean(x * x, axis=-1, keepdims=True) + EPS) * g


def _ffn_body(xe_ref, xn_ref, gpre_ref, gpost_ref, wg_ref, wu_ref, wd_ref, *rest, n_chunks, n_side):
    side_in, (o_ref, *side_out), (h_ref, acc_ref) = rest[:n_side], rest[n_side:2 * n_side + 1], rest[2 * n_side + 1:]
    r, j = pl.program_id(0), pl.program_id(1)
    last_r = pl.num_programs(0) - 1
    par = r % 2
    oth = 1 - par
    rc = xn_ref.shape[0]
    rows = pl.ds(pl.multiple_of(jnp.minimum(j, n_chunks - 1) * rc, rc), rc)
    g_pre = gpre_ref[...]
    g_post_half = 0.5 * gpost_ref[...]

    def prenorm_chunk():
        return _rms(xn_ref[...], g_pre).astype(BF16)

    def postnorm_chunk():
        return xe_ref[...] + _rms(acc_ref[par, rows, :], g_post_half)

    @pl.when(r == 0)
    def _():
        @pl.when(j == 0)
        def _():
            acc_ref[...] = jnp.zeros_like(acc_ref)
        h_ref[par, rows, :] = prenorm_chunk()

    def matmul_step(with_chunk):
        if with_chunk:
            out_chunk = postnorm_chunk()
            h_chunk = prenorm_chunk()
        h = h_ref[oth]
        g = jnp.dot(h, wg_ref[...], preferred_element_type=F32)
        u = jnp.dot(h, wu_ref[...], preferred_element_type=F32)
        a = (g * jax.nn.sigmoid(g) * u).astype(BF16)
        acc_ref[oth] = jnp.where(j == 0, 0.0, acc_ref[oth]) + jnp.dot(a, wd_ref[...], preferred_element_type=F32)
        if with_chunk:
            o_ref[...] = out_chunk
            h_ref[par, rows, :] = h_chunk
        for src_ref, dst_ref in zip(side_in, side_out):
            dst_ref[...] = src_ref[...].astype(BF16)

    matmul_row = (r > 0) & (r < last_r)
    pl.when(matmul_row & (j < n_chunks))(functools.partial(matmul_step, True))
    pl.when(matmul_row & (j >= n_chunks))(functools.partial(matmul_step, False))

    @pl.when(r == last_r)
    def _():
        o_ref[...] = postnorm_chunk()


def _side_rows(n_rows, max_blocks):
    rb = 16
    while n_rows % rb or n_rows // rb > max_blocks:
        rb += 16
    return rb


def _ffn(x, gpre, gpost, wg, wu, wd, layer, side=None, tm=1024, tf=512, rows_per_step=128):
    t, d = x.shape
    f = wg.shape[-1]
    tm = min(tm, t)
    nt, nj = t // tm, f // tf
    n_chunks = tm // rows_per_step
    assert n_chunks <= nj, "every row chunk of a tile needs its own d_ff step"
    n_blocks = nt * n_chunks
    chunk = lambda j: jnp.minimum(j, n_chunks - 1)
    emit_map = lambda r, j: (jnp.clip((r - 2) * n_chunks + chunk(j), 0, n_blocks - 1), 0)
    next_map = lambda r, j: (jnp.minimum(r * n_chunks + chunk(j), n_blocks - 1), 0)
    wj = lambda r, j: jnp.where(r == 0, 0, jnp.where(r == nt + 1, nj - 1, j))

    side_w, side_layer = side if side is not None else ((), 0)
    side_in_specs, side_out_specs, side_shapes = [], [], []
    for w in side_w:
        _, n_rows, n_cols = w.shape
        rb = _side_rows(n_rows, nt * nj)
        blk = lambda r, j, nb=n_rows // rb: jnp.clip((r - 1) * nj + j, 0, nb - 1)
        side_in_specs.append(pl.BlockSpec((None, rb, n_cols), lambda r, j, blk=blk: (side_layer, blk(r, j), 0)))
        side_out_specs.append(pl.BlockSpec((rb, n_cols), lambda r, j, blk=blk: (blk(r, j), 0)))
        side_shapes.append(jax.ShapeDtypeStruct((n_rows, n_cols), BF16))

    out, *side_out = pl.pallas_call(
        functools.partial(_ffn_body, n_chunks=n_chunks, n_side=len(side_w)),
        grid=(nt + 2, nj),
        in_specs=[
            pl.BlockSpec((rows_per_step, d), emit_map),
            pl.BlockSpec((rows_per_step, d), next_map),
            pl.BlockSpec((None, 1, d), lambda r, j: (layer, 0, 0)),
            pl.BlockSpec((None, 1, d), lambda r, j: (layer, 0, 0)),
            pl.BlockSpec((d, tf), lambda r, j: (0, wj(r, j))),
            pl.BlockSpec((d, tf), lambda r, j: (0, wj(r, j))),
            pl.BlockSpec((tf, d), lambda r, j: (wj(r, j), 0)),
        ] + side_in_specs,
        out_specs=[pl.BlockSpec((rows_per_step, d), emit_map)] + side_out_specs,
        out_shape=[jax.ShapeDtypeStruct((t, d), F32)] + side_shapes,
        scratch_shapes=[pltpu.VMEM((2, tm, d), BF16), pltpu.VMEM((2, tm, d), F32)],
        compiler_params=_cparams(("arbitrary", "arbitrary")),
        name="ffn",
    )(x, x, gpre, gpost, wg, wu, wd, *side_w)
    return out, tuple(side_out)


def _inproj_body(x_ref, g_ref, w_ref, wgate_ref, o_ref, og_ref, h_ref):
    r = pl.program_id(0)
    j = pl.program_id(1)
    par = r % 2
    rc = x_ref.shape[0]
    rows = pl.ds(pl.multiple_of(j * rc, rc), rc)
    h_chunk = _rms(x_ref[...], g_ref[...]).astype(BF16)
    og_ref[...] = jnp.dot(h_chunk, wgate_ref[...], preferred_element_type=F32)

    @pl.when(r > 0)
    def _():
        o_ref[...] = jnp.dot(h_ref[1 - par], w_ref[...], preferred_element_type=F32).astype(BF16)

    h_ref[par, rows, :] = h_chunk


def _inproj(x, g, w, wgate, layer, n, tm=512, tn=2816):
    t, d = x.shape
    tm = min(tm, t)
    nt, nj = t // tm, n // tn
    rc = tm // nj
    chunk_map = lambda r, j: (jnp.minimum(r * nj + j, nt * nj - 1), 0)
    return pl.pallas_call(
        _inproj_body,
        grid=(nt + 1, nj),
        in_specs=[
            pl.BlockSpec((rc, d), chunk_map),
            pl.BlockSpec((None, 1, d), lambda r, j: (layer, 0, 0)),
            pl.BlockSpec((None, d, tn), lambda r, j: (layer, 0, j)),
            pl.BlockSpec((None, d, GATE_LANES), lambda r, j: (layer, 0, 0)),
        ],
        out_specs=[
            pl.BlockSpec((tm, tn), lambda r, j: (jnp.maximum(r - 1, 0), jnp.where(r == 0, 0, j))),
            pl.BlockSpec((rc, GATE_LANES), chunk_map),
        ],
        out_shape=[
            jax.ShapeDtypeStruct((t, n), BF16),
            jax.ShapeDtypeStruct((t, GATE_LANES), F32),
        ],
        scratch_shapes=[pltpu.VMEM((2, tm, d), BF16)],
        compiler_params=_cparams(("arbitrary", "arbitrary")),
        name="inproj",
    )(x, g, w, wgate)


def _log_sigmoid(x):
    return -(jnp.maximum(-x, 0.0) + jnp.log1p(jnp.exp(-jnp.abs(x))))


def _split3(x):
    hi = x.astype(BF16)
    r1 = x - hi.astype(F32)
    mid = r1.astype(BF16)
    lo = (r1 - mid.astype(F32)).astype(BF16)
    return hi, mid, lo


def _gateprep_body(g_ref, bias_ref, gcol_ref, grow_ref, *, nch):
    L = M_CHUNK
    r = lax.broadcasted_iota(jnp.int32, (2 * L, L), 0)
    c = lax.broadcasted_iota(jnp.int32, (2 * L, L), 1)
    tri = jnp.where(((r < L) & (c <= r)) | ((r >= L) & (c >= r - L)), 1.0, 0.0).astype(BF16)
    lane = lax.broadcasted_iota(jnp.int32, (L, GATE_LANES), 1)
    for ch in range(nch):
        x = g_ref[0, ch * L:(ch + 1) * L, :] + bias_ref[...]
        cs = None
        for part in _split3(_log_sigmoid(x)):
            d = jnp.dot(tri, part, preferred_element_type=F32)
            cs = d if cs is None else cs + d
        out = jnp.where(lane < 2 * M_HEADS, x, jnp.where(lane < 3 * M_HEADS, cs[:L], cs[L:]))
        gcol_ref[0, ch * L:(ch + 1) * L, :] = out
        grow_ref[0, :, ch * L:(ch + 1) * L] = out.T[0:N_GATES, :]


def _gateprep(gates, bias, rows=1024):
    b, s, _ = gates.shape
    rows = min(rows, s)
    return pl.pallas_call(
        functools.partial(_gateprep_body, nch=rows // M_CHUNK),
        grid=(b, s // rows),
        in_specs=[
            pl.BlockSpec((1, rows, GATE_LANES), lambda i, j: (i, j, 0)),
            pl.BlockSpec((1, GATE_LANES), lambda i, j: (0, 0)),
        ],
        out_specs=[
            pl.BlockSpec((1, rows, GATE_LANES), lambda i, j: (i, j, 0)),
            pl.BlockSpec((1, N_GATES, rows), lambda i, j: (i, 0, j)),
        ],
        out_shape=[
            jax.ShapeDtypeStruct((b, s, GATE_LANES), F32),
            jax.ShapeDtypeStruct((b, N_GATES, s), F32),
        ],
        compiler_params=_cparams(("parallel", "parallel")),
        name="gateprep",
    )(gates, bias)


def _conv_body(xp_ref, xc_ref, xn_ref, w_ref, o_ref, xe_ref, *, ts):
    i = pl.program_id(1)
    c = pl.program_id(2)
    H, RB, WIN = CONV_HALO, CONV_ROW_BLOCK, CONV_WINDOW
    pad = CONV_WIDTH // 2
    tc = xc_ref.shape[-1]
    xe_ref[0:H, :] = jnp.where(i == 0, 0.0, xp_ref[0]).astype(BF16)
    xe_ref[H:H + ts, :] = xc_ref[0]
    xe_ref[H + ts:H + ts + H, :] = jnp.where(i == pl.num_programs(1) - 1, 0.0, xn_ref[0]).astype(BF16)
    xe_ref[H + ts + H:, :] = jnp.zeros((xe_ref.shape[0] - (ts + 2 * H), tc), BF16)

    taps = [k for k in range(CONV_WIDTH) if k != pad]
    diag = (lax.broadcasted_iota(jnp.int32, (RB, WIN), 1) - lax.broadcasted_iota(jnp.int32, (RB, WIN), 0))
    shift = jnp.concatenate([jnp.where(diag == H - pad + k, 1.0, 0.0).astype(BF16) for k in taps], axis=0)

    scale = jnp.where(c >= pl.num_programs(2) // 2, M_HEAD_DIM ** -0.5, 1.0)
    for b in range(ts // RB):
        window = xe_ref[b * RB:b * RB + WIN, :]
        shifted = jnp.dot(shift, window, preferred_element_type=F32)
        acc = xc_ref[0, b * RB:(b + 1) * RB, :].astype(F32) * w_ref[pad:pad + 1, :]
        for t, k in enumerate(taps):
            acc = acc + shifted[t * RB:(t + 1) * RB, :] * w_ref[k:k + 1, :]
        y = acc * jax.nn.sigmoid(acc)
        o_ref[0, b * RB:(b + 1) * RB, :] = (y * scale).astype(BF16)


def _conv_silu(proj, conv_w, layer, col0, ts=1024, tc=512):
    b, s, _ = proj.shape
    ts = min(ts, s)
    width = conv_w.shape[-1]
    cb0 = col0 // tc
    hb = ts // CONV_HALO
    nhb = s // CONV_HALO
    assert CONV_WINDOW >= CONV_ROW_BLOCK + 2 * CONV_HALO and ts % CONV_ROW_BLOCK == 0
    return pl.pallas_call(
        functools.partial(_conv_body, ts=ts),
        grid=(b, s // ts, width // tc),
        in_specs=[
            pl.BlockSpec((1, CONV_HALO, tc), lambda bi, i, c: (bi, jnp.maximum(i * hb - 1, 0), cb0 + c)),
            pl.BlockSpec((1, ts, tc), lambda bi, i, c: (bi, i, cb0 + c)),
            pl.BlockSpec((1, CONV_HALO, tc), lambda bi, i, c: (bi, jnp.minimum((i + 1) * hb, nhb - 1), cb0 + c)),
            pl.BlockSpec((None, CONV_WIDTH, tc), lambda bi, i, c: (layer, 0, c)),
        ],
        out_specs=pl.BlockSpec((1, ts, tc), lambda bi, i, c: (bi, i, c)),
        out_shape=jax.ShapeDtypeStruct((b, s, width), BF16),
        scratch_shapes=[pltpu.VMEM((ts - CONV_ROW_BLOCK + CONV_WINDOW, tc), BF16)],
        compiler_params=_cparams(("parallel", "parallel", "parallel")),
        name="conv_silu",
    )(proj, proj, proj, conv_w)


def _rope(x_bf16, cos, sin_signed):
    n = x_bf16.shape[-1] // ATT_HEAD_DIM
    outs = []
    for h in range(n):
        x = x_bf16[:, h * ATT_HEAD_DIM:(h + 1) * ATT_HEAD_DIM].astype(F32)
        y = x * cos + pltpu.roll(x, ATT_HEAD_DIM // 2, axis=1) * sin_signed
        outs.append(y.astype(BF16))
    return outs


def _attn_body(sink_ref, q_ref, kp_ref, kc_ref, kn_ref, vp_ref, vc_ref, vn_ref,
               cp_ref, cc_ref, cn_ref, sp_ref, sc_ref, sn_ref, o_ref, *, tq):
    i = pl.program_id(1)
    W = ATT_BLOCK
    nsb = tq // W
    qh = _rope(q_ref[0], cc_ref[...], sc_ref[...])
    kp = _rope(kp_ref[0], cp_ref[...], sp_ref[...])
    kc = _rope(kc_ref[0], cc_ref[...], sc_ref[...])
    kn = _rope(kn_ref[0], cn_ref[...], sn_ref[...])
    kwin = [jnp.concatenate([kp[h], kc[h], kn[h]], axis=0) for h in range(ATT_KV_HEADS)]
    vall = jnp.concatenate([vp_ref[0], vc_ref[0], vn_ref[0]], axis=0)
    r = lax.broadcasted_iota(jnp.int32, (W, W), 0)
    c = lax.broadcasted_iota(jnp.int32, (W, W), 1)
    first_block = i == 0
    last_block = i == pl.num_programs(1) - 1
    scale = ATT_HEAD_DIM ** -0.5 * LOG2_E
    units = [(sb, hk) for sb in range(nsb) for hk in range(ATT_KV_HEADS)]

    def masks(sb):
        left = c >= r
        right = c <= r
        if sb == 0:
            left = left & jnp.logical_not(first_block)
        if sb == nsb - 1:
            right = right & jnp.logical_not(last_block)
        return left, right

    scores = []
    for sb, hk in units:
        q4 = jnp.concatenate(
            [qh[hk * ATT_GROUP + g][sb * W:(sb + 1) * W, :] for g in range(ATT_GROUP)], axis=0)
        kw = kwin[hk][sb * W:(sb + 3) * W, :]
        scores.append(lax.dot_general(q4, kw, (((1,), (1,)), ((), ())), preferred_element_type=F32) * scale)

    probs, dens = [], []
    for (sb, hk), s in zip(units, scores):
        left, right = masks(sb)
        ps, ds = [], []
        for g in range(ATT_GROUP):
            sink = sink_ref[hk * ATT_GROUP + g] * LOG2_E
            sg = s[g * W:(g + 1) * W, :]
            sl = jnp.where(left, sg[:, :W], -jnp.inf)
            sm = sg[:, W:2 * W]
            sr = jnp.where(right, sg[:, 2 * W:], -jnp.inf)
            m = jnp.maximum(jnp.max(jnp.maximum(jnp.maximum(sl, sm), sr), axis=-1, keepdims=True), sink)
            pl_, pm, pr = jnp.exp2(sl - m), jnp.exp2(sm - m), jnp.exp2(sr - m)
            ds.append(jnp.sum(pl_ + pm + pr, axis=-1, keepdims=True) + jnp.exp2(sink - m))
            ps.append(jnp.concatenate([pl_, pm, pr], axis=1).astype(BF16))
        probs.append(jnp.concatenate(ps, axis=0))
        dens.append(ds)

    outs = []
    for (sb, hk), p in zip(units, probs):
        vw = vall[sb * W:(sb + 3) * W, hk * ATT_HEAD_DIM:(hk + 1) * ATT_HEAD_DIM]
        outs.append(jnp.dot(p, vw, preferred_element_type=F32))

    for (sb, hk), pv, ds in zip(units, outs, dens):
        for g in range(ATT_GROUP):
            h = hk * ATT_GROUP + g
            o_ref[0, sb * W:(sb + 1) * W, h * ATT_HEAD_DIM:(h + 1) * ATT_HEAD_DIM] = (
                pv[g * W:(g + 1) * W, :] / ds[g]).astype(BF16)


def _attention(proj, cos2, sin2, sink, tq=512):
    b, s, _ = proj.shape
    tq = min(tq, s)
    W = ATT_BLOCK
    aw = ATT_HEADS * ATT_HEAD_DIM
    kvw = ATT_KV_HEADS * ATT_HEAD_DIM
    kb, vb = aw // kvw, aw // kvw + 1
    nb = tq // W
    nwb = s // W
    prev = lambda i: jnp.maximum(i * nb - 1, 0)
    nxt = lambda i: jnp.minimum((i + 1) * nb, nwb - 1)
    return pl.pallas_call(
        functools.partial(_attn_body, tq=tq),
        grid=(b, s // tq),
        in_specs=[
            pl.BlockSpec(memory_space=pltpu.SMEM),
            pl.BlockSpec((1, tq, aw), lambda bi, i: (bi, i, 0)),
            pl.BlockSpec((1, W, kvw), lambda bi, i: (bi, prev(i), kb)),
            pl.BlockSpec((1, tq, kvw), lambda bi, i: (bi, i, kb)),
            pl.BlockSpec((1, W, kvw), lambda bi, i: (bi, nxt(i), kb)),
            pl.BlockSpec((1, W, kvw), lambda bi, i: (bi, prev(i), vb)),
            pl.BlockSpec((1, tq, kvw), lambda bi, i: (bi, i, vb)),
            pl.BlockSpec((1, W, kvw), lambda bi, i: (bi, nxt(i), vb)),
            pl.BlockSpec((W, ATT_HEAD_DIM), lambda bi, i: (prev(i), 0)),
            pl.BlockSpec((tq, ATT_HEAD_DIM), lambda bi, i: (i, 0)),
            pl.BlockSpec((W, ATT_HEAD_DIM), lambda bi, i: (nxt(i), 0)),
            pl.BlockSpec((W, ATT_HEAD_DIM), lambda bi, i: (prev(i), 0)),
            pl.BlockSpec((tq, ATT_HEAD_DIM), lambda bi, i: (i, 0)),
            pl.BlockSpec((W, ATT_HEAD_DIM), lambda bi, i: (nxt(i), 0)),
        ],
        out_specs=pl.BlockSpec((1, tq, aw), lambda bi, i: (bi, i, 0)),
        out_shape=jax.ShapeDtypeStruct((b, s, aw), BF16),
        compiler_params=_cparams(("parallel", "parallel")),
        name="attention",
    )(sink, proj, proj, proj, proj, proj, proj, proj, cos2, cos2, cos2, sin2, sin2, sin2)


def _rep2(x):
    return jnp.concatenate([x, x], axis=1)


def _mlstm_body(qf_ref, kf_ref, vf0_ref, vf1_ref, gcf_ref, grf_ref,
                qb_ref, kb_ref, vb0_ref, vb1_ref, gcb_ref, grb_ref,
                hf_ref, hb_ref, st_ref, m_ref, *, chains_per_group):
    @pl.when(pl.program_id(1) == 0)
    def _():
        st_ref[...] = jnp.zeros_like(st_ref)
        m_ref[...] = jnp.zeros_like(m_ref)

    L = M_CHUNK
    n_sub = qf_ref.shape[1] // L
    dirs = ((qf_ref, kf_ref, (vf0_ref, vf1_ref), gcf_ref, grf_ref, hf_ref),
            (qb_ref, kb_ref, (vb0_ref, vb1_ref), gcb_ref, grb_ref, hb_ref))
    r = lax.broadcasted_iota(jnp.int32, (L, L), 0)
    c = lax.broadcasted_iota(jnp.int32, (L, L), 1)
    tris = (c <= r, c >= r)
    all_chains = [(rev, h) for rev in range(2) for h in range(M_HEADS)]
    for sub in range(n_sub):
        rows = (slice(sub * L, (sub + 1) * L), slice((n_sub - 1 - sub) * L, (n_sub - sub) * L))
        for g0 in range(0, len(all_chains), chains_per_group):
            _mlstm_group(all_chains[g0:g0 + chains_per_group], dirs, rows, tris, st_ref, m_ref)


def _mlstm_group(chains, dirs, rows, tris, st_ref, m_ref):
    D = M_HEAD_DIM
    R = M_NCOL
    L = tris[0].shape[0]
    slot = [rev * M_HEADS + h for rev, h in chains]
    q, k, v, bc, igc, arow, m_prev = [], [], [], [], [], [], []
    for rev, h in chains:
        q_ref, k_ref, v_refs, gc_ref, gr_ref, _ = dirs[rev]
        rs = rows[rev]
        ig_lane = rev * M_HEADS + h
        b_lane = (2 + rev) * M_HEADS + h
        q.append(q_ref[0, rs, h * D:(h + 1) * D])
        k.append(k_ref[0, rs, h * D:(h + 1) * D])
        v.append(v_refs[h // 2][0, rs, (h % 2) * D:(h % 2 + 1) * D])
        bc.append(jnp.broadcast_to(gc_ref[0, rs, b_lane:b_lane + 1], (L, R)))
        igc.append(jnp.broadcast_to(gc_ref[0, rs, ig_lane:ig_lane + 1], (L, R)))
        arow.append(gr_ref[0, ig_lane:ig_lane + 1, rs] - gr_ref[0, b_lane:b_lane + 1, rs])
        m_prev.append(m_ref[rev * M_HEADS + h][0:1, :])

    n = len(chains)
    log_d = [jnp.where(tris[chains[i][0]], _rep2(bc[i]) + arow[i], -jnp.inf) for i in range(n)]
    row_max = [jnp.broadcast_to(jnp.max(log_d[i], axis=-1, keepdims=True), (L, R)) for i in range(n)]
    log_inter = [bc[i] + m_prev[i] for i in range(n)]
    m_t = [jnp.maximum(log_inter[i], row_max[i]) for i in range(n)]
    d_mat = [jnp.exp(log_d[i] - _rep2(m_t[i])) for i in range(n)]
    inter = [jnp.exp(log_inter[i] - m_t[i]) for i in range(n)]
    qk = [lax.dot_general(q[i], k[i], (((1,), (1,)), ((), ())), preferred_element_type=F32) for i in range(n)]
    s = [qk[i] * d_mat[i] for i in range(n)]
    p2 = [jnp.dot(q[i], st_ref[slot[i]].astype(BF16), preferred_element_type=F32) for i in range(n)]
    sv = [jnp.dot(s[i].astype(BF16), v[i], preferred_element_type=F32) for i in range(n)]
    for i, (rev, h) in enumerate(chains):
        num = sv[i] + _rep2(inter[i]) * p2[i][:, :D]
        den = jnp.broadcast_to(jnp.sum(s[i], axis=-1, keepdims=True), (L, R)) + inter[i] * p2[i][:, D:]
        scale = 1.0 / jnp.maximum(jnp.abs(den), jnp.exp(-m_t[i]))
        dirs[rev][5][0, rows[rev], h * D:(h + 1) * D] = num * _rep2(scale)

    for i, (rev, h) in enumerate(chains):
        b_last = bc[i][0:1, :] if rev else bc[i][L - 1:L, :]
        log_w = b_last - bc[i] + igc[i]
        m_new = jnp.maximum(b_last + m_prev[i], jnp.max(log_w, axis=0, keepdims=True))
        w = jnp.exp(log_w - m_new)
        decay = jnp.exp(b_last + m_prev[i] - m_new)
        wext = jnp.concatenate([(_rep2(w) * v[i].astype(F32)).astype(BF16), w.astype(BF16)], axis=1)
        upd = lax.dot_general(k[i], wext, (((0,), (0,)), ((), ())), preferred_element_type=F32)
        st_ref[slot[i]] = jnp.concatenate([decay, decay, decay], axis=1) * st_ref[slot[i]] + upd
        m_ref[slot[i]] = jnp.broadcast_to(m_new, m_ref.shape[1:])


def _mlstm(qk, proj, gcol, grow, vcol0, chains_per_group=2, chunks_per_step=2):
    b, s, _ = qk.shape
    L = M_CHUNK * chunks_per_step
    nc = s // L
    mw = M_HEADS * M_HEAD_DIM
    vw = mw // 2
    vb = vcol0 // vw
    fwd = lambda i: i
    bwd = lambda i: nc - 1 - i

    def specs(pos):
        return [
            pl.BlockSpec((1, L, mw), lambda bi, i: (bi, pos(i), 0)),
            pl.BlockSpec((1, L, mw), lambda bi, i: (bi, pos(i), 1)),
            pl.BlockSpec((1, L, vw), lambda bi, i: (bi, pos(i), vb)),
            pl.BlockSpec((1, L, vw), lambda bi, i: (bi, pos(i), vb + 1)),
            pl.BlockSpec((1, L, GATE_LANES), lambda bi, i: (bi, pos(i), 0)),
            pl.BlockSpec((1, N_GATES, L), lambda bi, i: (bi, 0, pos(i))),
        ]

    nchain = 2 * M_HEADS
    return pl.pallas_call(
        functools.partial(_mlstm_body, chains_per_group=chains_per_group),
        grid=(b, nc),
        in_specs=specs(fwd) + specs(bwd),
        out_specs=[
            pl.BlockSpec((1, L, mw), lambda bi, i: (bi, fwd(i), 0)),
            pl.BlockSpec((1, L, mw), lambda bi, i: (bi, bwd(i), 0)),
        ],
        out_shape=[jax.ShapeDtypeStruct((b, s, mw), F32), jax.ShapeDtypeStruct((b, s, mw), F32)],
        scratch_shapes=[
            pltpu.VMEM((nchain, M_HEAD_DIM, M_HEAD_DIM + M_NCOL), F32),
            pltpu.VMEM((nchain, 8, 128), F32),
        ],
        compiler_params=_cparams(("parallel", "arbitrary")),
        name="mlstm",
    )(qk, qk, proj, proj, gcol, grow, qk, qk, proj, proj, gcol, grow)


def _outproj_body(x_ref, ya_ref, hf_ref, hb_ref, o0_ref, o1_ref, mn_ref, w_ref, g_ref, o_ref):
    D = M_HEAD_DIM
    hm = hf_ref[...] + hb_ref[...]
    ys = []
    for h in range(M_HEADS):
        o_gate = (o0_ref, o1_ref)[h // 2][:, (h % 2) * D:(h % 2 + 1) * D].astype(F32)
        y = _rms(hm[:, h * D:(h + 1) * D], mn_ref[:, h * D:(h + 1) * D])
        ys.append((jax.nn.sigmoid(o_gate) * y).astype(BF16))
    ym = jnp.concatenate(ys, axis=-1)
    aw = ya_ref.shape[-1]
    m = jnp.dot(ya_ref[...], w_ref[0:aw, :], preferred_element_type=F32)
    m = m + jnp.dot(ym, w_ref[aw:, :], preferred_element_type=F32)
    o_ref[...] = x_ref[...] + _rms(m, g_ref[...])


def _outproj(x, ya, hf, hb, proj, mnorm, w, g, layer, ocol0, tm=512):
    t, d = x.shape
    aw = ya.shape[-1]
    mw = hf.shape[-1]
    ow = mw // 2
    ob = ocol0 // ow
    return pl.pallas_call(
        _outproj_body,
        grid=(t // tm,),
        in_specs=[
            pl.BlockSpec((tm, d), lambda i: (i, 0)),
            pl.BlockSpec((tm, aw), lambda i: (i, 0)),
            pl.BlockSpec((tm, mw), lambda i: (i, 0)),
            pl.BlockSpec((tm, mw), lambda i: (i, 0)),
            pl.BlockSpec((tm, ow), lambda i: (i, ob)),
            pl.BlockSpec((tm, ow), lambda i: (i, ob + 1)),
            pl.BlockSpec((None, 1, mw), lambda i: (layer, 0, 0)),
            pl.BlockSpec((None, aw + mw, d), lambda i: (layer, 0, 0)),
            pl.BlockSpec((None, 1, d), lambda i: (layer, 0, 0)),
        ],
        out_specs=pl.BlockSpec((tm, d), lambda i: (i, 0)),
        out_shape=jax.ShapeDtypeStruct((t, d), F32),
        compiler_params=_cparams(("parallel",)),
        name="outproj",
    )(x, ya, hf, hb, proj, proj, mnorm, w, g)


def _rope_tables(seq):
    half = ATT_HEAD_DIM // 2
    pos = jnp.arange(seq, dtype=F32)
    inv_freq = ROPE_THETA ** (-jnp.arange(half, dtype=F32) / half)
    ang = pos[:, None] * inv_freq[None, :]
    cos, sin = jnp.cos(ang), jnp.sin(ang)
    return jnp.concatenate([cos, cos], axis=-1), jnp.concatenate([-sin, sin], axis=-1)


def _mixer(x2, b, s, layer, mix_norm_pre, mix_norm_post, w_main, w_gate, b_gate_pad, conv_w,
           attn_sink, mlstm_norm, w_out, cos2, sin2):
    t, d = x2.shape
    att_w = ATT_HEADS * ATT_HEAD_DIM
    kv_w = ATT_KV_HEADS * ATT_HEAD_DIM
    m_w = M_HEADS * M_HEAD_DIM
    qm0 = att_w + 2 * kv_w
    vm0 = qm0 + 2 * m_w
    om0 = vm0 + m_w
    proj, gates = _inproj(x2, mix_norm_pre, w_main, w_gate, layer, qm0 + 4 * m_w)
    proj3 = proj.reshape(b, s, proj.shape[-1])
    gcol, grow = _gateprep(gates.reshape(b, s, GATE_LANES), b_gate_pad[layer])
    qk = _conv_silu(proj3, conv_w, layer, qm0)
    ya = _attention(proj3, cos2, sin2, attn_sink[layer])
    hf, hb = _mlstm(qk, proj3, gcol, grow, vm0)
    return _outproj(x2, ya.reshape(t, att_w), hf.reshape(t, m_w), hb.reshape(t, m_w), proj,
                    mlstm_norm, w_out, mix_norm_post, layer, om0)


def kernel(x, ffn1_norm_pre, ffn1_norm_post, ffn1_w_gate, ffn1_w_up, ffn1_w_down, mix_norm_pre, mix_norm_post, w_in, b_gate, conv_w, attn_sink, mlstm_norm, w_out, ffn2_norm_pre, ffn2_norm_post, ffn2_w_gate, ffn2_w_up, ffn2_w_down):
    b, s, d = x.shape
    depth = w_in.shape[0]
    n_main = w_in.shape[-1] - N_GATES
    vec = lambda g: g.reshape(depth, 1, g.shape[-1])
    bf = lambda w: w.astype(BF16)
    w_main = bf(w_in)
    w_gate = bf(jnp.pad(w_in[:, :, n_main:], ((0, 0), (0, 0), (0, GATE_LANES - N_GATES))))
    b_gate_pad = jnp.pad(b_gate, ((0, 0), (0, GATE_LANES - N_GATES))).reshape(depth, 1, GATE_LANES)
    ffn_norms = ((vec(ffn1_norm_pre), vec(ffn1_norm_post)), (vec(ffn2_norm_pre), vec(ffn2_norm_post)))
    ffn_w32 = ((ffn1_w_gate, ffn1_w_up, ffn1_w_down), (ffn2_w_gate, ffn2_w_up, ffn2_w_down))
    w_out_b = bf(w_out)
    cos2, sin2 = _rope_tables(s)
    x2 = x.reshape(b * s, d)
    ffn_w = tuple(bf(w[0]) for w in ffn_w32[0])
    for layer in range(depth):
        x2, ffn_w = _ffn(x2, *ffn_norms[0], *ffn_w, layer, side=(ffn_w32[1], layer))
        x2 = _mixer(x2, b, s, layer, vec(mix_norm_pre), vec(mix_norm_post), w_main, w_gate, b_gate_pad,
                    conv_w, attn_sink, vec(mlstm_norm), w_out_b, cos2, sin2)
        side = (ffn_w32[0], layer + 1) if layer + 1 < depth else None
        x2, ffn_w = _ffn(x2, *ffn_norms[1], *ffn_w, layer, side=side)
    return x2.reshape(b, s, d)
```
